```python
import jax, jax.numpy as jnp
from jax import lax
import numpy as np

D_MODEL = 1024
BATCH = 32
SEQ = 2048
DEPTH = 1
DEC_BATCH = 8
DEC_SEQ = 4096
PAST_LEN = 128

HEAD_DIM = 128
A_Q_HEADS = 8
A_KV_HEADS = 2
A_HALF_WINDOW = 128
A_BLOCK = 128
B_GROUPS = ((128, 1), (512, 4), (2048, 16))
B_HEADS = 4
B_BLOCK = 64
D_FF = 4 * D_MODEL
ROPE_THETA = 10000.0
EPS = 1e-6
NEG = -1e30

A_Q = A_Q_HEADS * HEAD_DIM
A_KV = A_KV_HEADS * HEAD_DIM
B_W = B_HEADS * HEAD_DIM
N_B = len(B_GROUPS)
D_IN = A_Q + 2 * A_KV + 3 * N_B * B_W + 2 * D_MODEL

kernel_name = "hybrid_window_dilated_encoder"


def rmsnorm(x, g):
    xf = x.astype(jnp.float32)
    y = xf * lax.rsqrt(jnp.mean(xf * xf, axis=-1, keepdims=True) + EPS)
    return (y * g.astype(jnp.float32)).astype(x.dtype)


def rope(t, pos):
    half = t.shape[-1] // 2
    inv = jnp.power(ROPE_THETA, -jnp.arange(half, dtype=jnp.float32) / half)
    ang = pos.astype(jnp.float32)[:, None] * inv[None, :]
    cos = jnp.cos(ang)[None, :, None, :]
    sin = jnp.sin(ang)[None, :, None, :]
    tf = t.astype(jnp.float32)
    t1, t2 = tf[..., :half], tf[..., half:]
    return jnp.concatenate([t1 * cos - t2 * sin, t1 * sin + t2 * cos], axis=-1).astype(t.dtype)


def banded_attention(q, k, v, half_window, block, sink=None):
    bsz, L, H, hd = q.shape
    hkv = k.shape[2]
    grp = H // hkv
    nb = -(-L // block)
    Lp = nb * block
    pad = Lp - L
    q = jnp.pad(q * (hd ** -0.5), ((0, 0), (0, pad), (0, 0), (0, 0)))
    k = jnp.pad(k, ((0, 0), (block, block + pad), (0, 0), (0, 0)))
    v = jnp.pad(v, ((0, 0), (block, block + pad), (0, 0), (0, 0)))
    qb = q.reshape(bsz, nb, block, hkv, grp, hd)

    def bands(t):
        t = t.reshape(bsz, nb + 2, block, hkv, hd)
        return jnp.concatenate([t[:, :-2], t[:, 1:-1], t[:, 2:]], axis=2)

    kb, vb = bands(k), bands(v)
    s = jnp.einsum('bnqhgd,bnkhd->bnhgqk', qb, kb).astype(jnp.float32)
    qpos = jnp.arange(Lp).reshape(nb, block, 1)
    kpos = (jnp.arange(nb)[:, None, None] - 1) * block + jnp.arange(3 * block)[None, None, :]
    mask = (jnp.abs(kpos - qpos) <= half_window) & (kpos >= 0) & (kpos < L)
    s = jnp.where(mask[None, :, None, None], s, NEG)
    m = jnp.max(s, axis=-1)
    if sink is not None:
        sk = sink.astype(jnp.float32).reshape(hkv, grp)[None, None, :, :, None]
        m = jnp.maximum(m, sk)
    p = jnp.exp(s - m[..., None])
    den = jnp.sum(p, axis=-1)
    if sink is not None:
        den = den + jnp.exp(sk - m)
    o = jnp.einsum('bnhgqk,bnkhd->bnqhgd', p, vb.astype(jnp.float32))
    o = o / jnp.transpose(den, (0, 1, 4, 2, 3))[..., None]
    o = o.reshape(bsz, Lp, H, hd)[:, :L].astype(v.dtype)
    lse = jnp.transpose(m + jnp.log(den), (0, 1, 4, 2, 3)).reshape(bsz, Lp, H)[:, :L]
    return o, lse


def dilated_attention(q, k, v, window, dilation):
    bsz, L, H, hd = q.shape
    Ld = L // dilation

    def fold(t):
        return jnp.transpose(t.reshape(bsz, Ld, dilation, H, hd), (0, 2, 1, 3, 4)).reshape(bsz * dilation, Ld, H, hd)

    o, lse = banded_attention(fold(q), fold(k), fold(v), (window // 2) // dilation, B_BLOCK)
    o = jnp.transpose(o.reshape(bsz, dilation, Ld, H, hd), (0, 2, 1, 3, 4)).reshape(bsz, L, H, hd)
    lse = jnp.transpose(lse.reshape(bsz, dilation, Ld, H), (0, 2, 1, 3)).reshape(bsz, L, H)
    return o, lse


def encoder_layer(x, w_in, sink, w_a, w_b, w_o, g_pre_mix, g_post_mix, g_pre_mlp, g_post_mlp, w_1, w_2):
    bsz, L, _ = x.shape
    pos = jnp.arange(L)
    h = rmsnorm(x, g_pre_mix)
    z = h @ w_in
    widths = [A_Q, A_KV, A_KV] + [B_W] * (3 * N_B) + [D_MODEL]
    parts = jnp.split(z, [int(c) for c in np.cumsum(widths)], axis=-1)

    qa = rope(parts[0].reshape(bsz, L, A_Q_HEADS, HEAD_DIM), pos)
    ka = rope(parts[1].reshape(bsz, L, A_KV_HEADS, HEAD_DIM), pos)
    va = parts[2].reshape(bsz, L, A_KV_HEADS, HEAD_DIM)
    oa, _ = banded_attention(qa, ka, va, A_HALF_WINDOW, A_BLOCK, sink)
    ya = oa.reshape(bsz, L, A_Q) @ w_a

    outs, lses = [], []
    for g, (window, dilation) in enumerate(B_GROUPS):
        qg = rope(parts[3 + 3 * g].reshape(bsz, L, B_HEADS, HEAD_DIM), pos)
        kg = rope(parts[4 + 3 * g].reshape(bsz, L, B_HEADS, HEAD_DIM), pos)
        vg = parts[5 + 3 * g].reshape(bsz, L, B_HEADS, HEAD_DIM)
        o, lse = dilated_attention(qg, kg, vg, window, dilation)
        outs.append(o)
        lses.append(lse)
    wts = jax.nn.softmax(jnp.stack(lses), axis=0)
    ob = jnp.sum(wts[..., None] * jnp.stack(outs).astype(jnp.float32), axis=0).astype(x.dtype)
    yb = ob.reshape(bsz, L, B_W) @ w_b

    gate_a, gate_b = parts[-2], parts[-1]
    mix = (jax.nn.sigmoid(gate_a) * ya + jax.nn.sigmoid(gate_b) * yb) @ w_o
    x = x + rmsnorm(mix, g_post_mix)

    u = jax.nn.relu(rmsnorm(x, g_pre_mlp) @ w_1)
    x = x + rmsnorm((u * u) @ w_2, g_post_mlp)
    return x


def trunk(x, w_in, sink, w_a, w_b, w_o, g_pre_mix, g_post_mix, g_pre_mlp, g_post_mlp, w_1, w_2):
    for l in range(DEPTH):
        x = encoder_layer(x, w_in[l], sink[l], w_a[l], w_b[l], w_o[l], g_pre_mix[l], g_post_mix[l],
                          g_pre_mlp[l], g_post_mlp[l], w_1[l], w_2[l])
    return x


def setup_inputs(seed: int = 0) -> dict:
    key = jax.random.key(seed)
    ks = jax.random.split(key, 14)
    f32 = jnp.float32

    def nrm(k, shape, scale):
        return jax.random.normal(k, shape, f32) * scale

    return {
        "x_prompt": nrm(ks[0], (BATCH, SEQ, D_MODEL), 1.0),
        "x_sample": nrm(ks[1], (DEC_BATCH, DEC_SEQ, D_MODEL), 1.0),
        "w_in": nrm(ks[2], (DEPTH, D_MODEL, D_IN), D_MODEL ** -0.5),
        "sink": nrm(ks[3], (DEPTH, A_Q_HEADS), 0.5),
        "w_a": nrm(ks[4], (DEPTH, A_Q, D_MODEL), A_Q ** -0.5),
        "w_b": nrm(ks[5], (DEPTH, B_W, D_MODEL), B_W ** -0.5),
        "w_o": nrm(ks[6], (DEPTH, D_MODEL, D_MODEL), D_MODEL ** -0.5),
        "g_pre_mix": 1.0 + nrm(ks[7], (DEPTH, D_MODEL), 0.1),
        "g_post_mix": 1.0 + nrm(ks[8], (DEPTH, D_MODEL), 0.1),
        "g_pre_mlp": 1.0 + nrm(ks[9], (DEPTH, D_MODEL), 0.1),
        "g_post_mlp": 1.0 + nrm(ks[10], (DEPTH, D_MODEL), 0.1),
        "w_1": nrm(ks[11], (DEPTH, D_MODEL, D_FF), D_MODEL ** -0.5),
        "w_2": nrm(ks[12], (DEPTH, D_FF, D_MODEL), D_FF ** -0.5),
    }


def reference(x_prompt, x_sample, w_in, sink, w_a, w_b, w_o, g_pre_mix, g_post_mix, g_pre_mlp, g_post_mlp, w_1, w_2):
    y_prompt = trunk(x_prompt, w_in, sink, w_a, w_b, w_o, g_pre_mix, g_post_mix, g_pre_mlp, g_post_mlp, w_1, w_2)
    y_sample = trunk(x_sample, w_in, sink, w_a, w_b, w_o, g_pre_mix, g_post_mix, g_pre_mlp, g_post_mlp, w_1, w_2)
    return (y_prompt, y_sample)
```

```python
import functools

import jax
import jax.numpy as jnp
from jax import lax
from jax.experimental import pallas as pl
from jax.experimental.pallas import tpu as pltpu

D_MODEL = 1024
HEAD_DIM = 128
A_Q_HEADS = 8
A_KV_HEADS = 2
A_GROUP = A_Q_HEADS // A_KV_HEADS
A_HALF_WINDOW = 128
B_DILATIONS = (1, 4, 16)
B_HEADS = 4
B_HALF_WINDOW = 64
D_FF = 4 * D_MODEL
ROPE_THETA = 10000.0
EPS = 1e-6
NEG = -1e30

A_Q = A_Q_HEADS * HEAD_DIM
A_KV = A_KV_HEADS * HEAD_DIM
B_W = B_HEADS * HEAD_DIM
N_B = len(B_DILATIONS)
D_IN = A_Q + 2 * A_KV + 3 * N_B * B_W + 2 * D_MODEL

BF16 = jnp.bfloat16
F32 = jnp.float32

PROJ_TM = 512
POST_TM = 256
A_QBLOCK = 128
A_KBAND = 3 * A_QBLOCK
B_QBLOCK = 128
PROJ_CHUNK = 512
VMEM_LIMIT = 56 * 1024 * 1024


def _resident(shape):
    nd = len(shape)
    return pl.BlockSpec(shape, lambda *_: (0,) * nd, pipeline_mode=pl.Buffered(1))


def _rms(x, g):
    return x * lax.rsqrt(jnp.mean(x * x, axis=-1, keepdims=True) + EPS) * g


def _proj_kernel(x_ref, g_ref, w_ref, cos_ref, sin_ref,
                 qa_ref, ka_ref, va_ref,
                 q0_ref, k0_ref, v0_ref, q1_ref, k1_ref, v1_ref, q2_ref, k2_ref, v2_ref,
                 gate_ref, fold_ref):
    tm = x_ref.shape[0]
    h = _rms(x_ref[...], g_ref[...]).astype(BF16)
    cosf = cos_ref[...]
    sinf = sin_ref[...]
    scale = HEAD_DIM ** -0.5

    def rope(t):
        return t * cosf + pltpu.roll(t, HEAD_DIM // 2, 1) * sinf

    def chunk(c):
        return jnp.dot(h, w_ref[:, c * PROJ_CHUNK:(c + 1) * PROJ_CHUNK],
                       preferred_element_type=F32)

    def heads(z):
        return [z[:, j * HEAD_DIM:(j + 1) * HEAD_DIM] for j in range(PROJ_CHUNK // HEAD_DIM)]

    for c in range(2):
        for j, t in enumerate(heads(chunk(c))):
            col = (c * 4 + j) * HEAD_DIM
            qa_ref[:, col:col + HEAD_DIM] = (rope(t) * scale).astype(BF16)
    zkv = heads(chunk(2))
    for j in range(A_KV_HEADS):
        ka_ref[:, j * HEAD_DIM:(j + 1) * HEAD_DIM] = rope(zkv[j]).astype(BF16)
        va_ref[:, j * HEAD_DIM:(j + 1) * HEAD_DIM] = zkv[A_KV_HEADS + j].astype(BF16)

    group_refs = ((q0_ref, k0_ref, v0_ref), (q1_ref, k1_ref, v1_ref), (q2_ref, k2_ref, v2_ref))
    for g, dil in enumerate(B_DILATIONS):
        for part in range(3):
            out_ref = group_refs[g][part]
            for j, t in enumerate(heads(chunk(3 + 3 * g + part))):
                if part == 0:
                    t = rope(t) * scale
                elif part == 1:
                    t = rope(t)
                lanes = slice(j * HEAD_DIM, (j + 1) * HEAD_DIM)
                if dil == 1:
                    out_ref[:, lanes] = t.astype(BF16)
                else:
                    fold_ref[j] = t
                    for r in range(dil):
                        rows = fold_ref[j, pl.ds(r, tm // dil, stride=dil), :]
                        out_ref[0, r, :, lanes] = rows.astype(BF16)

    for c in range(4):
        gate_ref[:, c * PROJ_CHUNK:(c + 1) * PROJ_CHUNK] = chunk(12 + c).astype(BF16)


def _project(x, g_pre, w_in, cosf, sinf):
    bsz, seq, _ = x.shape
    tm = PROJ_TM
    tiles = seq // tm
    n_tok = bsz * seq
    x2 = x.reshape(n_tok, D_MODEL)

    def row(i):
        return (i, 0)

    def folded(dil):
        return pl.BlockSpec((1, dil, tm // dil, B_W), lambda i: (i // tiles, 0, i % tiles, 0))

    out_shape = [jax.ShapeDtypeStruct((n_tok, A_Q), BF16),
                 jax.ShapeDtypeStruct((n_tok, A_KV), BF16),
                 jax.ShapeDtypeStruct((n_tok, A_KV), BF16)]
    out_specs = [pl.BlockSpec((tm, A_Q), row), pl.BlockSpec((tm, A_KV), row),
                 pl.BlockSpec((tm, A_KV), row)]
    for dil in B_DILATIONS:
        for _ in range(3):
            if dil == 1:
                out_shape.append(jax.ShapeDtypeStruct((n_tok, B_W), BF16))
                out_specs.append(pl.BlockSpec((tm, B_W), row))
            else:
                out_shape.append(jax.ShapeDtypeStruct((bsz, dil, seq // dil, B_W), BF16))
                out_specs.append(folded(dil))
    out_shape.append(jax.ShapeDtypeStruct((n_tok, 2 * D_MODEL), BF16))
    out_specs.append(pl.BlockSpec((tm, 2 * D_MODEL), row))

    return pl.pallas_call(
        _proj_kernel,
        grid=(n_tok // tm,),
        in_specs=[pl.BlockSpec((tm, D_MODEL), row),
                  _resident((1, D_MODEL)),
                  _resident((D_MODEL, D_IN)),
                  pl.BlockSpec((tm, HEAD_DIM), lambda i: (i % tiles, 0)),
                  pl.BlockSpec((tm, HEAD_DIM), lambda i: (i % tiles, 0))],
        out_specs=out_specs,
        out_shape=out_shape,
        scratch_shapes=[pltpu.VMEM((B_HEADS, tm, HEAD_DIM), F32)],
        compiler_params=pltpu.CompilerParams(dimension_semantics=("arbitrary",),
                                             vmem_limit_bytes=VMEM_LIMIT),
        name="proj",
    )(x2, g_pre, w_in, cosf, sinf)


def _attn_a_kernel(sink_ref, q_ref, k_ref, v_ref, o_ref):
    seq = k_ref.shape[1]
    i = pl.program_id(1)
    q0 = i * A_QBLOCK
    kstart = pl.multiple_of(jnp.clip(q0 - A_QBLOCK, 0, seq - A_KBAND), A_QBLOCK)
    rel = (lax.broadcasted_iota(jnp.int32, (A_QBLOCK, A_KBAND), 1)
           - lax.broadcasted_iota(jnp.int32, (A_QBLOCK, A_KBAND), 0)) + (kstart - q0)
    mask = (jnp.abs(rel) <= A_HALF_WINDOW)[None]
    for j in range(A_KV_HEADS):
        lanes = slice(j * HEAD_DIM, (j + 1) * HEAD_DIM)
        kb = k_ref[0, pl.ds(kstart, A_KBAND), lanes]
        vb = v_ref[0, pl.ds(kstart, A_KBAND), lanes]
        hs = [j * A_GROUP + g for g in range(A_GROUP)]
        q4 = jnp.concatenate([q_ref[0, :, h * HEAD_DIM:(h + 1) * HEAD_DIM] for h in hs], axis=0)
        s = lax.dot_general(q4, kb, (((1,), (1,)), ((), ())), preferred_element_type=F32)
        s = jnp.where(mask, s.reshape(A_GROUP, A_QBLOCK, A_KBAND), NEG)
        sk = jnp.concatenate([jnp.full((1, 1, 1), sink_ref[h], F32) for h in hs], axis=0)
        m = jnp.maximum(jnp.max(s, axis=-1, keepdims=True), sk)
        p = jnp.exp(s - m)
        den = jnp.sum(p, axis=-1, keepdims=True) + jnp.exp(sk - m)
        o = jnp.dot(p.reshape(A_GROUP * A_QBLOCK, A_KBAND).astype(BF16), vb,
                    preferred_element_type=F32)
        o = o.reshape(A_GROUP, A_QBLOCK, HEAD_DIM) / den
        for g, h in enumerate(hs):
            o_ref[0, :, h * HEAD_DIM:(h + 1) * HEAD_DIM] = o[g].astype(BF16)


def _attention_a(sink, qa, ka, va, bsz, seq):
    q3 = qa.reshape(bsz, seq, A_Q)
    k3 = ka.reshape(bsz, seq, A_KV)
    v3 = va.reshape(bsz, seq, A_KV)
    return pl.pallas_call(
        _attn_a_kernel,
        grid=(bsz, seq // A_QBLOCK),
        in_specs=[pl.BlockSpec(memory_space=pltpu.SMEM),
                  pl.BlockSpec((1, A_QBLOCK, A_Q), lambda b, i: (b, i, 0)),
                  pl.BlockSpec((1, seq, A_KV), lambda b, i: (b, 0, 0)),
                  pl.BlockSpec((1, seq, A_KV), lambda b, i: (b, 0, 0))],
        out_specs=pl.BlockSpec((1, A_QBLOCK, A_Q), lambda b, i: (b, i, 0)),
        out_shape=jax.ShapeDtypeStruct((bsz, seq, A_Q), BF16),
        compiler_params=pltpu.CompilerParams(dimension_semantics=("arbitrary", "arbitrary"),
                                             vmem_limit_bytes=VMEM_LIMIT),
        name="attn_a",
    )(sink, q3, k3, v3)


def _attn_b_kernel(q0_ref, k0_ref, v0_ref, q1_ref, k1_ref, v1_ref, q2_ref, k2_ref, v2_ref,
                   o_ref, acc_ref, lse_ref):
    seq = o_ref.shape[1]
    group_refs = ((q0_ref, k0_ref, v0_ref), (q1_ref, k1_ref, v1_ref), (q2_ref, k2_ref, v2_ref))

    for g, dil in enumerate(B_DILATIONS):
        q_ref, k_ref, v_ref = group_refs[g]
        ld = seq // dil
        kw = min(2 * B_QBLOCK, ld)
        nq = ld // B_QBLOCK
        rel0 = (lax.broadcasted_iota(jnp.int32, (B_QBLOCK, kw), 1)
                - lax.broadcasted_iota(jnp.int32, (B_QBLOCK, kw), 0))

        def block(t, carry, dil=dil, kw=kw, nq=nq, ld=ld, rel0=rel0,
                  q_ref=q_ref, k_ref=k_ref, v_ref=v_ref, g=g):
            r = t // nq
            q0 = pl.multiple_of((t % nq) * B_QBLOCK, B_QBLOCK)
            kstart = pl.multiple_of(jnp.clip(q0 - B_HALF_WINDOW, 0, ld - kw), B_HALF_WINDOW)
            q = q_ref[0, r, pl.ds(q0, B_QBLOCK), :]
            kb = k_ref[0, r, pl.ds(kstart, kw), :]
            vb = v_ref[0, r, pl.ds(kstart, kw), :]
            s = lax.dot_general(q, kb, (((1,), (1,)), ((), ())), preferred_element_type=F32)
            s = jnp.where(jnp.abs(rel0 + (kstart - q0)) <= B_HALF_WINDOW, s, NEG)
            m = jnp.max(s, axis=-1, keepdims=True)
            p = jnp.exp(s - m)
            den = jnp.sum(p, axis=-1, keepdims=True)
            o = jnp.dot(p.astype(BF16), vb, preferred_element_type=F32) / den
            lse = jnp.broadcast_to(m + jnp.log(den), (B_QBLOCK, HEAD_DIM))
            if dil == 1:
                rows = pl.ds(q0, B_QBLOCK)
            else:
                rows = pl.ds(q0 * dil + r, B_QBLOCK, stride=dil)
            acc_ref[g, rows, :] = o
            lse_ref[g, rows, :] = lse
            return carry

        lax.fori_loop(0, dil * nq, block, 0)

    def merge(t, carry):
        rows = pl.ds(pl.multiple_of(t * B_QBLOCK, B_QBLOCK), B_QBLOCK)
        ls = [lse_ref[g, rows, :] for g in range(N_B)]
        top = jnp.maximum(jnp.maximum(ls[0], ls[1]), ls[2])
        es = [jnp.exp(l - top) for l in ls]
        tot = es[0] + es[1] + es[2]
        out = sum((es[g] / tot) * acc_ref[g, rows, :] for g in range(N_B))
        o_ref[0, rows, :] = out.astype(BF16)
        return carry

    lax.fori_loop(0, seq // B_QBLOCK, merge, 0)


def _attention_b(qkv, bsz, seq):
    args, in_specs = [], []
    for g, dil in enumerate(B_DILATIONS):
        ld = seq // dil
        for part in range(3):
            args.append(qkv[3 * g + part].reshape(bsz, dil, ld, B_W))
            in_specs.append(pl.BlockSpec((1, dil, ld, HEAD_DIM), lambda b, h: (b, 0, 0, h)))
    return pl.pallas_call(
        _attn_b_kernel,
        grid=(bsz, B_HEADS),
        in_specs=in_specs,
        out_specs=pl.BlockSpec((1, seq, HEAD_DIM), lambda b, h: (b, 0, h)),
        out_shape=jax.ShapeDtypeStruct((bsz, seq, B_W), BF16),
        scratch_shapes=[pltpu.VMEM((N_B, seq, HEAD_DIM), F32),
                        pltpu.VMEM((N_B, seq, HEAD_DIM), F32)],
        compiler_params=pltpu.CompilerParams(dimension_semantics=("arbitrary", "arbitrary"),
                                             vmem_limit_bytes=VMEM_LIMIT),
        name="attn_b",
    )(*args)


def _post_kernel(x_ref, oa_ref, ob_ref, gate_ref, wa_ref, wb_ref, wo_ref, w1_ref, w2_ref,
                 g_mix_ref, g_pre_ref, g_post_ref, y_ref):
    ya = jnp.dot(oa_ref[...], wa_ref[...], preferred_element_type=F32)
    yb = jnp.dot(ob_ref[...], wb_ref[...], preferred_element_type=F32)
    ga = gate_ref[:, :D_MODEL].astype(F32)
    gb = gate_ref[:, D_MODEL:].astype(F32)
    mix_in = jax.nn.sigmoid(ga) * ya + jax.nn.sigmoid(gb) * yb
    mix = jnp.dot(mix_in.astype(BF16), wo_ref[...], preferred_element_type=F32)
    x1 = x_ref[...] + _rms(mix, g_mix_ref[...])
    h2 = _rms(x1, g_pre_ref[...]).astype(BF16)
    acc = jnp.zeros_like(x1)
    ff_chunk = D_MODEL
    for c in range(D_FF // ff_chunk):
        cols = slice(c * ff_chunk, (c + 1) * ff_chunk)
        u = jnp.maximum(jnp.dot(h2, w1_ref[:, cols], preferred_element_type=F32), 0.0)
        acc = acc + jnp.dot((u * u).astype(BF16), w2_ref[cols, :], preferred_element_type=F32)
    y_ref[...] = x1 + _rms(acc, g_post_ref[...])


def _post(x, oa, ob, gates, w_a, w_b, w_o, w_1, w_2, g_mix, g_pre, g_post):
    bsz, seq, _ = x.shape
    n_tok = bsz * seq
    tm = POST_TM

    def row(i):
        return (i, 0)

    y = pl.pallas_call(
        _post_kernel,
        grid=(n_tok // tm,),
        in_specs=[pl.BlockSpec((tm, D_MODEL), row),
                  pl.BlockSpec((tm, A_Q), row),
                  pl.BlockSpec((tm, B_W), row),
                  pl.BlockSpec((tm, 2 * D_MODEL), row),
                  _resident((A_Q, D_MODEL)), _resident((B_W, D_MODEL)),
                  _resident((D_MODEL, D_MODEL)),
                  _resident((D_MODEL, D_FF)), _resident((D_FF, D_MODEL)),
                  _resident((1, D_MODEL)), _resident((1, D_MODEL)), _resident((1, D_MODEL))],
        out_specs=pl.BlockSpec((tm, D_MODEL), row),
        out_shape=jax.ShapeDtypeStruct((n_tok, D_MODEL), F32),
        compiler_params=pltpu.CompilerParams(dimension_semantics=("arbitrary",),
                                             vmem_limit_bytes=VMEM_LIMIT),
        name="post",
    )(x.reshape(n_tok, D_MODEL), oa.reshape(n_tok, A_Q), ob.reshape(n_tok, B_W), gates,
      w_a, w_b, w_o, w_1, w_2, g_mix, g_pre, g_post)
    return y.reshape(bsz, seq, D_MODEL)


def _rope_tables(seq):
    half = HEAD_DIM // 2
    inv = jnp.power(ROPE_THETA, -jnp.arange(half, dtype=F32) / half)
    ang = jnp.arange(seq).astype(F32)[:, None] * inv[None, :]
    cos, sin = jnp.cos(ang), jnp.sin(ang)
    return jnp.concatenate([cos, cos], axis=-1), jnp.concatenate([-sin, sin], axis=-1)


def _layer(x, w_in, sink, w_a, w_b, w_o, g_pre_mix, g_post_mix, g_pre_mlp, g_post_mlp, w_1, w_2):
    bsz, seq, _ = x.shape
    cosf, sinf = _rope_tables(seq)
    outs = _project(x, g_pre_mix.reshape(1, D_MODEL), w_in, cosf, sinf)
    qa, ka, va = outs[:3]
    qkv_b = outs[3:12]
    gates = outs[12]
    oa = _attention_a(sink, qa, ka, va, bsz, seq)
    ob = _attention_b(qkv_b, bsz, seq)
    return _post(x, oa, ob, gates, w_a, w_b, w_o, w_1, w_2,
                 g_post_mix.reshape(1, D_MODEL), g_pre_mlp.reshape(1, D_MODEL),
                 g_post_mlp.reshape(1, D_MODEL))


def kernel(x_prompt, x_sample, w_in, sink, w_a, w_b, w_o, g_pre_mix, g_post_mix, g_pre_mlp,
           g_post_mlp, w_1, w_2):
    depth = w_in.shape[0]
    xs = [x_prompt, x_sample]
    for l in range(depth):
        ws = (w_in[l].astype(BF16), sink[l], w_a[l].astype(BF16), w_b[l].astype(BF16),
              w_o[l].astype(BF16), g_pre_mix[l], g_post_mix[l], g_pre_mlp[l], g_post_mlp[l],
              w_1[l].astype(BF16), w_2[l].astype(BF16))
        xs = [_layer(x, *ws) for x in xs]
    return (xs[0], xs[1])
```

```python
import functools

import jax
import jax.numpy as jnp
from jax import lax
from jax.experimental import pallas as pl
from jax.experimental.pallas import tpu as pltpu

D_MODEL = 1024
HEAD_DIM = 128
A_Q_HEADS = 8
A_KV_HEADS = 2
A_GROUP = A_Q_HEADS // A_KV_HEADS
A_HALF_WINDOW = 128
B_DILATIONS = (1, 4, 16)
B_HEADS = 4
B_HALF_WINDOW = 64
D_FF = 4 * D_MODEL
ROPE_THETA = 10000.0
EPS = 1e-6
NEG = -1e30

A_Q = A_Q_HEADS * HEAD_DIM
A_KV = A_KV_HEADS * HEAD_DIM
B_W = B_HEADS * HEAD_DIM
N_B = len(B_DILATIONS)
D_IN = A_Q + 2 * A_KV + 3 * N_B * B_W + 2 * D_MODEL

LOG2E = 1.4426950408889634
LN2 = 0.6931471805599453
Q_SCALE = HEAD_DIM ** -0.5 * LOG2E

BF16 = jnp.bfloat16
F32 = jnp.float32

PROJ_TM = 512
POST_TM = 256
A_QBLOCK = 128
A_KBAND = 3 * A_QBLOCK
A_CHUNK_HEADS = 2
A_STEP_BLOCKS = 4
B_QBLOCK = 256
MERGE_ROWS = 128
B_UNROLL = 8
PROJ_CHUNK = 512
VMEM_LIMIT = 56 * 1024 * 1024


def _resident(shape):
    nd = len(shape)
    return pl.BlockSpec(shape, lambda *_: (0,) * nd, pipeline_mode=pl.Buffered(1))


def _rms(x, g):
    return x * lax.rsqrt(jnp.mean(x * x, axis=-1, keepdims=True) + EPS) * g


def _proj_kernel(x_ref, g_ref, w_ref, cos_ref, sin_ref,
                 qa_ref, ka_ref, vat_ref,
                 q0_ref, k0_ref, v0_ref, q1_ref, k1_ref, v1_ref, q2_ref, k2_ref, v2_ref,
                 gate_ref, fold_ref):
    tm = x_ref.shape[0]
    h = _rms(x_ref[...], g_ref[...]).astype(BF16)
    cosf = cos_ref[...]
    sinf = sin_ref[...]
    scale = Q_SCALE

    def rope(t):
        return t * cosf + pltpu.roll(t, HEAD_DIM // 2, 1) * sinf

    def chunk(c):
        return jnp.dot(h, w_ref[:, c * PROJ_CHUNK:(c + 1) * PROJ_CHUNK],
                       preferred_element_type=F32)

    def heads(z):
        return [z[:, j * HEAD_DIM:(j + 1) * HEAD_DIM] for j in range(PROJ_CHUNK // HEAD_DIM)]

    for c in range(2):
        for j, t in enumerate(heads(chunk(c))):
            col = (c * 4 + j) * HEAD_DIM
            qa_ref[:, col:col + HEAD_DIM] = (rope(t) * scale).astype(BF16)
    zkv = heads(chunk(2))
    for j in range(A_KV_HEADS):
        ka_ref[:, j * HEAD_DIM:(j + 1) * HEAD_DIM] = rope(zkv[j]).astype(BF16)
        vat_ref[0, j * HEAD_DIM:(j + 1) * HEAD_DIM, :] = zkv[A_KV_HEADS + j].T.astype(BF16)

    group_refs = ((q0_ref, k0_ref, v0_ref), (q1_ref, k1_ref, v1_ref), (q2_ref, k2_ref, v2_ref))
    for g, dil in enumerate(B_DILATIONS):
        for part in range(3):
            out_ref = group_refs[g][part]
            for j, t in enumerate(heads(chunk(3 + 3 * g + part))):
                if part == 0:
                    t = rope(t) * scale
                elif part == 1:
                    t = rope(t)
                lanes = slice(j * HEAD_DIM, (j + 1) * HEAD_DIM)
                if dil == 1:
                    out_ref[:, lanes] = t.astype(BF16)
                else:
                    fold_ref[j] = t
                    for r in range(dil):
                        rows = fold_ref[j, pl.ds(r, tm // dil, stride=dil), :]
                        out_ref[0, r, :, lanes] = rows.astype(BF16)

    for c in range(4):
        gate_ref[:, c * PROJ_CHUNK:(c + 1) * PROJ_CHUNK] = chunk(12 + c).astype(BF16)


def _project(x, g_pre, w_in, cosf, sinf):
    bsz, seq, _ = x.shape
    tm = PROJ_TM
    tiles = seq // tm
    n_tok = bsz * seq
    x2 = x.reshape(n_tok, D_MODEL)

    def row(i):
        return (i, 0)

    def folded(dil):
        return pl.BlockSpec((1, dil, tm // dil, B_W), lambda i: (i // tiles, 0, i % tiles, 0))

    out_shape = [jax.ShapeDtypeStruct((n_tok, A_Q), BF16),
                 jax.ShapeDtypeStruct((n_tok, A_KV), BF16),
                 jax.ShapeDtypeStruct((bsz, A_KV, seq), BF16)]
    out_specs = [pl.BlockSpec((tm, A_Q), row), pl.BlockSpec((tm, A_KV), row),
                 pl.BlockSpec((1, A_KV, tm), lambda i: (i // tiles, 0, i % tiles))]
    for dil in B_DILATIONS:
        for _ in range(3):
            if dil == 1:
                out_shape.append(jax.ShapeDtypeStruct((n_tok, B_W), BF16))
                out_specs.append(pl.BlockSpec((tm, B_W), row))
            else:
                out_shape.append(jax.ShapeDtypeStruct((bsz, dil, seq // dil, B_W), BF16))
                out_specs.append(folded(dil))
    out_shape.append(jax.ShapeDtypeStruct((n_tok, 2 * D_MODEL), BF16))
    out_specs.append(pl.BlockSpec((tm, 2 * D_MODEL), row))

    return pl.pallas_call(
        _proj_kernel,
        grid=(n_tok // tm,),
        in_specs=[pl.BlockSpec((tm, D_MODEL), row),
                  _resident((1, D_MODEL)),
                  _resident((D_MODEL, D_IN)),
                  pl.BlockSpec((tm, HEAD_DIM), lambda i: (i % tiles, 0)),
                  pl.BlockSpec((tm, HEAD_DIM), lambda i: (i % tiles, 0))],
        out_specs=out_specs,
        out_shape=out_shape,
        scratch_shapes=[pltpu.VMEM((B_HEADS, tm, HEAD_DIM), F32)],
        compiler_params=pltpu.CompilerParams(dimension_semantics=("arbitrary",),
                                             vmem_limit_bytes=VMEM_LIMIT),
        name="proj",
    )(x2, g_pre, w_in, cosf, sinf)


def _band_bias(n_keys, n_queries, half_window, step):
    rel = (jnp.arange(n_keys)[None, :, None] - jnp.arange(n_queries)[None, None, :]
           - step * jnp.arange(3)[:, None, None])
    return jnp.where(jnp.abs(rel) <= half_window, 0.0, NEG).astype(F32)


def _attn_a_kernel(sink_ref, bias_ref, q_ref, k_ref, vt_ref, o_ref):
    seq = k_ref.shape[1]
    step_q0 = pl.program_id(1) * (A_STEP_BLOCKS * A_QBLOCK)

    chunks = [(blk, j, [j * A_GROUP + c * A_CHUNK_HEADS + g for g in range(A_CHUNK_HEADS)])
              for blk in range(A_STEP_BLOCKS) for j in range(A_KV_HEADS)
              for c in range(A_GROUP // A_CHUNK_HEADS)]

    def window(blk):
        q0 = step_q0 + blk * A_QBLOCK
        kstart = pl.multiple_of(jnp.clip(q0 - A_QBLOCK, 0, seq - A_KBAND), A_QBLOCK)
        return q0, kstart

    def scores(blk, j, hs):
        _, kstart = window(blk)
        rows = slice(blk * A_QBLOCK, (blk + 1) * A_QBLOCK)
        kb = k_ref[0, pl.ds(kstart, A_KBAND), j * HEAD_DIM:(j + 1) * HEAD_DIM]
        qs = jnp.concatenate([q_ref[0, rows, h * HEAD_DIM:(h + 1) * HEAD_DIM] for h in hs],
                             axis=0)
        return lax.dot_general(kb, qs, (((1,), (1,)), ((), ())), preferred_element_type=F32)

    def softmax(st, blk, hs):
        q0, kstart = window(blk)
        bias = bias_ref[(q0 - kstart) // A_QBLOCK]
        st = st + jnp.concatenate([bias] * A_CHUNK_HEADS, axis=1)
        sk = jnp.concatenate([jnp.full((1, A_QBLOCK), sink_ref[h] * LOG2E, F32) for h in hs],
                             axis=1)
        m = jnp.maximum(jnp.max(st, axis=0, keepdims=True), sk)
        pt = jnp.exp2(st - m)
        den = jnp.sum(pt, axis=0, keepdims=True) + jnp.exp2(sk - m)
        return pt.astype(BF16), 1.0 / den

    def output(blk, j, hs, pt, rden):
        _, kstart = window(blk)
        rows = slice(blk * A_QBLOCK, (blk + 1) * A_QBLOCK)
        vtb = vt_ref[0, j * HEAD_DIM:(j + 1) * HEAD_DIM, pl.ds(kstart, A_KBAND)]
        ot = jnp.dot(vtb, pt, preferred_element_type=F32) * rden
        for g, h in enumerate(hs):
            o_ref[0, rows, h * HEAD_DIM:(h + 1) * HEAD_DIM] = (
                ot[:, g * A_QBLOCK:(g + 1) * A_QBLOCK].T.astype(BF16))

    n = len(chunks)
    st = {0: scores(*chunks[0])}
    sm = {}
    for c in range(n):
        if c + 1 < n:
            st[c + 1] = scores(*chunks[c + 1])
        sm[c] = softmax(st.pop(c), chunks[c][0], chunks[c][2])
        if c >= 1:
            output(*chunks[c - 1], *sm.pop(c - 1))
    output(*chunks[n - 1], *sm.pop(n - 1))


def _attention_a(sink, qa, ka, vat, bsz, seq):
    q3 = qa.reshape(bsz, seq, A_Q)
    k3 = ka.reshape(bsz, seq, A_KV)
    v3 = vat
    return pl.pallas_call(
        _attn_a_kernel,
        grid=(bsz, seq // (A_STEP_BLOCKS * A_QBLOCK)),
        in_specs=[pl.BlockSpec(memory_space=pltpu.SMEM),
                  _resident((3, A_KBAND, A_QBLOCK)),
                  pl.BlockSpec((1, A_STEP_BLOCKS * A_QBLOCK, A_Q), lambda b, i: (b, i, 0)),
                  pl.BlockSpec((1, seq, A_KV), lambda b, i: (b, 0, 0)),
                  pl.BlockSpec((1, A_KV, seq), lambda b, i: (b, 0, 0))],
        out_specs=pl.BlockSpec((1, A_STEP_BLOCKS * A_QBLOCK, A_Q), lambda b, i: (b, i, 0)),
        out_shape=jax.ShapeDtypeStruct((bsz, seq, A_Q), BF16),
        compiler_params=pltpu.CompilerParams(dimension_semantics=("arbitrary", "arbitrary"),
                                             vmem_limit_bytes=VMEM_LIMIT),
        name="attn_a",
    )(sink, _band_bias(A_KBAND, A_QBLOCK, A_HALF_WINDOW, A_QBLOCK), q3, k3, v3)


def _b_geometry(seq, dil):
    ld = seq // dil
    qb = min(B_QBLOCK, ld)
    kw = min(qb + 2 * B_HALF_WINDOW, ld)
    return ld, qb, kw, ld // qb


def _attn_b_kernel(bias0_ref, bias1_ref, bias2_ref,
                   q0_ref, k0_ref, v0_ref, q1_ref, k1_ref, v1_ref, q2_ref, k2_ref, v2_ref,
                   o_ref, acc_ref, lse_ref):
    seq = o_ref.shape[1]
    group_refs = ((bias0_ref, q0_ref, k0_ref, v0_ref), (bias1_ref, q1_ref, k1_ref, v1_ref),
                  (bias2_ref, q2_ref, k2_ref, v2_ref))

    for g, dil in enumerate(B_DILATIONS):
        bias_ref, q_ref, k_ref, v_ref = group_refs[g]
        ld, qb, kw, nq = _b_geometry(seq, dil)

        n_blocks = dil * nq
        per_iter = min(B_UNROLL, n_blocks)

        def window(t, qb=qb, kw=kw, nq=nq, ld=ld):
            r = t // nq
            q0 = pl.multiple_of((t % nq) * qb, qb)
            kstart = pl.multiple_of(jnp.clip(q0 - B_HALF_WINDOW, 0, ld - kw), B_HALF_WINDOW)
            return r, q0, kstart

        def scores(t, window=window, qb=qb, kw=kw, q_ref=q_ref, k_ref=k_ref):
            r, q0, kstart = window(t)
            q = q_ref[0, r, pl.ds(q0, qb), :]
            kb = k_ref[0, r, pl.ds(kstart, kw), :]
            return lax.dot_general(kb, q, (((1,), (1,)), ((), ())), preferred_element_type=F32)

        def softmax(st, t, window=window, bias_ref=bias_ref):
            _, q0, kstart = window(t)
            st = st + bias_ref[(q0 - kstart) // B_HALF_WINDOW]
            m = jnp.max(st, axis=0, keepdims=True)
            pt = jnp.exp2(st - m)
            den = jnp.sum(pt, axis=0, keepdims=True)
            return pt.astype(BF16), 1.0 / den, m + jnp.log2(den)

        def output(t, pt, rden, lse2, window=window, dil=dil, qb=qb, kw=kw, v_ref=v_ref, g=g):
            r, q0, kstart = window(t)
            vb = v_ref[0, r, pl.ds(kstart, kw), :]
            ot = lax.dot_general(vb, pt, (((0,), (0,)), ((), ())),
                                 preferred_element_type=F32) * rden
            if dil == 1:
                rows = pl.ds(q0, qb)
            else:
                rows = pl.ds(q0 * dil + r, qb, stride=dil)
            acc_ref[g, rows, :] = ot.T
            lse_ref[g, rows, :] = jnp.broadcast_to(lse2, (HEAD_DIM, qb)).T

        def blocks(it, carry, per_iter=per_iter, scores=scores, softmax=softmax, output=output):
            ts = [it * per_iter + u for u in range(per_iter)]
            st = {0: scores(ts[0])}
            sm = {}
            for u in range(per_iter):
                if u + 1 < per_iter:
                    st[u + 1] = scores(ts[u + 1])
                sm[u] = softmax(st.pop(u), ts[u])
                if u >= 1:
                    output(ts[u - 1], *sm.pop(u - 1))
            output(ts[per_iter - 1], *sm.pop(per_iter - 1))
            return carry

        if n_blocks == per_iter:
            blocks(0, 0)
        else:
            lax.fori_loop(0, n_blocks // per_iter, blocks, 0)

    def merge(t, carry):
        rows = pl.ds(pl.multiple_of(t * MERGE_ROWS, MERGE_ROWS), MERGE_ROWS)
        ls = [lse_ref[g, rows, :] for g in range(N_B)]
        top = jnp.maximum(jnp.maximum(ls[0], ls[1]), ls[2])
        es = [jnp.exp2(l - top) for l in ls]
        tot = es[0] + es[1] + es[2]
        out = sum((es[g] / tot) * acc_ref[g, rows, :] for g in range(N_B))
        o_ref[0, rows, :] = out.astype(BF16)
        return carry

    lax.fori_loop(0, seq // MERGE_ROWS, merge, 0)


def _attention_b(qkv, bsz, seq):
    args, in_specs = [], []
    for dil in B_DILATIONS:
        _, qb, kw, _ = _b_geometry(seq, dil)
        args.append(_band_bias(kw, qb, B_HALF_WINDOW, B_HALF_WINDOW))
        in_specs.append(_resident((3, kw, qb)))
    for g, dil in enumerate(B_DILATIONS):
        ld = seq // dil
        for part in range(3):
            args.append(qkv[3 * g + part].reshape(bsz, dil, ld, B_W))
            in_specs.append(pl.BlockSpec((1, dil, ld, HEAD_DIM), lambda b, h: (b, 0, 0, h)))
    return pl.pallas_call(
        _attn_b_kernel,
        grid=(bsz, B_HEADS),
        in_specs=in_specs,
        out_specs=pl.BlockSpec((1, seq, HEAD_DIM), lambda b, h: (b, 0, h)),
        out_shape=jax.ShapeDtypeStruct((bsz, seq, B_W), BF16),
        scratch_shapes=[pltpu.VMEM((N_B, seq, HEAD_DIM), F32),
                        pltpu.VMEM((N_B, seq, HEAD_DIM), F32)],
        compiler_params=pltpu.CompilerParams(dimension_semantics=("arbitrary", "arbitrary"),
                                             vmem_limit_bytes=VMEM_LIMIT),
        name="attn_b",
    )(*args)


def _post_kernel(x_ref, oa_ref, ob_ref, gate_ref, wa_ref, wb_ref, wo_ref, w1_ref, w2_ref,
                 g_mix_ref, g_pre_ref, g_post_ref, y_ref):
    ya = jnp.dot(oa_ref[...], wa_ref[...], preferred_element_type=F32)
    yb = jnp.dot(ob_ref[...], wb_ref[...], preferred_element_type=F32)
    ga = gate_ref[:, :D_MODEL].astype(F32)
    gb = gate_ref[:, D_MODEL:].astype(F32)
    mix_in = jax.nn.sigmoid(ga) * ya + jax.nn.sigmoid(gb) * yb
    mix = jnp.dot(mix_in.astype(BF16), wo_ref[...], preferred_element_type=F32)
    x1 = x_ref[...] + _rms(mix, g_mix_ref[...])
    h2 = _rms(x1, g_pre_ref[...]).astype(BF16)
    acc = jnp.zeros_like(x1)
    ff_chunk = D_MODEL
    for c in range(D_FF // ff_chunk):
        cols = slice(c * ff_chunk, (c + 1) * ff_chunk)
        u = jnp.maximum(jnp.dot(h2, w1_ref[:, cols], preferred_element_type=F32), 0.0)
        acc = acc + jnp.dot((u * u).astype(BF16), w2_ref[cols, :], preferred_element_type=F32)
    y_ref[...] = x1 + _rms(acc, g_post_ref[...])


def _post(x, oa, ob, gates, w_a, w_b, w_o, w_1, w_2, g_mix, g_pre, g_post):
    bsz, seq, _ = x.shape
    n_tok = bsz * seq
    tm = POST_TM

    def row(i):
        return (i, 0)

    y = pl.pallas_call(
        _post_kernel,
        grid=(n_tok // tm,),
        in_specs=[pl.BlockSpec((tm, D_MODEL), row),
                  pl.BlockSpec((tm, A_Q), row),
                  pl.BlockSpec((tm, B_W), row),
                  pl.BlockSpec((tm, 2 * D_MODEL), row),
                  _resident((A_Q, D_MODEL)), _resident((B_W, D_MODEL)),
                  _resident((D_MODEL, D_MODEL)),
                  _resident((D_MODEL, D_FF)), _resident((D_FF, D_MODEL)),
                  _resident((1, D_MODEL)), _resident((1, D_MODEL)), _resident((1, D_MODEL))],
        out_specs=pl.BlockSpec((tm, D_MODEL), row),
        out_shape=jax.ShapeDtypeStruct((n_tok, D_MODEL), F32),
        compiler_params=pltpu.CompilerParams(dimension_semantics=("arbitrary",),
                                             vmem_limit_bytes=VMEM_LIMIT),
        name="post",
    )(x.reshape(n_tok, D_MODEL), oa.reshape(n_tok, A_Q), ob.reshape(n_tok, B_W), gates,
      w_a, w_b, w_o, w_1, w_2, g_mix, g_pre, g_post)
    return y.reshape(bsz, seq, D_MODEL)


def _rope_tables(seq):
    half = HEAD_DIM // 2
    inv = jnp.power(ROPE_THETA, -jnp.arange(half, dtype=F32) / half)
    ang = jnp.arange(seq).astype(F32)[:, None] * inv[None, :]
    cos, sin = jnp.cos(ang), jnp.sin(ang)
    return jnp.concatenate([cos, cos], axis=-1), jnp.concatenate([-sin, sin], axis=-1)


def _layer(x, w_in, sink, w_a, w_b, w_o, g_pre_mix, g_post_mix, g_pre_mlp, g_post_mlp, w_1, w_2):
    bsz, seq, _ = x.shape
    cosf, sinf = _rope_tables(seq)
    outs = _project(x, g_pre_mix.reshape(1, D_MODEL), w_in, cosf, sinf)
    qa, ka, va = outs[:3]
    qkv_b = outs[3:12]
    gates = outs[12]
    oa = _attention_a(sink, qa, ka, va, bsz, seq)
    ob = _attention_b(qkv_b, bsz, seq)
    return _post(x, oa, ob, gates, w_a, w_b, w_o, w_1, w_2,
                 g_post_mix.reshape(1, D_MODEL), g_pre_mlp.reshape(1, D_MODEL),
                 g_post_mlp.reshape(1, D_MODEL))


def kernel(x_prompt, x_sample, w_in, sink, w_a, w_b, w_o, g_pre_mix, g_post_mix, g_pre_mlp,
           g_post_mlp, w_1, w_2):
    depth = w_in.shape[0]
    xs = [x_prompt, x_sample]
    for l in range(depth):
        ws = (w_in[l].astype(BF16), sink[l], w_a[l].astype(BF16), w_b[l].astype(BF16),
              w_o[l].astype(BF16), g_pre_mix[l], g_post_mix[l], g_pre_mlp[l], g_post_mlp[l],
              w_1[l].astype(BF16), w_2[l].astype(BF16))
        xs = [_layer(x, *ws) for x in xs]
    return (xs[0], xs[1])
```

```python
import functools

import jax
import jax.numpy as jnp
from jax import lax
from jax.experimental import pallas as pl
from jax.experimental.pallas import tpu as pltpu

D_MODEL = 1024
HEAD_DIM = 128
A_Q_HEADS = 8
A_KV_HEADS = 2
A_GROUP = A_Q_HEADS // A_KV_HEADS
A_HALF_WINDOW = 128
B_DILATIONS = (1, 4, 16)
B_HEADS = 4
B_HALF_WINDOW = 64
D_FF = 4 * D_MODEL
ROPE_THETA = 10000.0
EPS = 1e-6
NEG = -1e30

A_Q = A_Q_HEADS * HEAD_DIM
A_KV = A_KV_HEADS * HEAD_DIM
B_W = B_HEADS * HEAD_DIM
N_B = len(B_DILATIONS)
D_IN = A_Q + 2 * A_KV + 3 * N_B * B_W + 2 * D_MODEL

LOG2E = 1.4426950408889634
LN2 = 0.6931471805599453
Q_SCALE = HEAD_DIM ** -0.5 * LOG2E

BF16 = jnp.bfloat16
F32 = jnp.float32

PROJ_TM = 512
POST_TM = 512
A_QBLOCK = 128
A_KBAND = 3 * A_QBLOCK
A_CHUNK_HEADS = 2
A_STEP_BLOCKS = 8
B_QBLOCK = 256
MERGE_ROWS = 128
B_UNROLL = 8
PROJ_CHUNK = 512
VMEM_LIMIT = 56 * 1024 * 1024


def _resident(shape):
    nd = len(shape)
    return pl.BlockSpec(shape, lambda *_: (0,) * nd, pipeline_mode=pl.Buffered(1))


def _rms(x, g):
    return x * lax.rsqrt(jnp.mean(x * x, axis=-1, keepdims=True) + EPS) * g


def _proj_kernel(x_ref, g_ref, w_ref, cos_ref, sin_ref,
                 qa_ref, ka_ref, vat_ref,
                 q0_ref, k0_ref, v0_ref, q1_ref, k1_ref, v1_ref, q2_ref, k2_ref, v2_ref,
                 gate_ref, fold_ref):
    tm = x_ref.shape[0]
    h = _rms(x_ref[...], g_ref[...]).astype(BF16)
    cosf = cos_ref[...]
    sinf = sin_ref[...]
    scale = Q_SCALE

    def rope(t):
        return t * cosf + pltpu.roll(t, HEAD_DIM // 2, 1) * sinf

    def chunk(c):
        return jnp.dot(h, w_ref[:, c * PROJ_CHUNK:(c + 1) * PROJ_CHUNK],
                       preferred_element_type=F32)

    def heads(z):
        return [z[:, j * HEAD_DIM:(j + 1) * HEAD_DIM] for j in range(PROJ_CHUNK // HEAD_DIM)]

    for c in range(2):
        for j, t in enumerate(heads(chunk(c))):
            col = (c * 4 + j) * HEAD_DIM
            qa_ref[:, col:col + HEAD_DIM] = (rope(t) * scale).astype(BF16)
    zkv = heads(chunk(2))
    for j in range(A_KV_HEADS):
        ka_ref[:, j * HEAD_DIM:(j + 1) * HEAD_DIM] = rope(zkv[j]).astype(BF16)
        vat_ref[0, j * HEAD_DIM:(j + 1) * HEAD_DIM, :] = zkv[A_KV_HEADS + j].T.astype(BF16)

    group_refs = ((q0_ref, k0_ref, v0_ref), (q1_ref, k1_ref, v1_ref), (q2_ref, k2_ref, v2_ref))
    for g, dil in enumerate(B_DILATIONS):
        for part in range(3):
            out_ref = group_refs[g][part]
            for j, t in enumerate(heads(chunk(3 + 3 * g + part))):
                if part == 0:
                    t = rope(t) * scale
                elif part == 1:
                    t = rope(t)
                lanes = slice(j * HEAD_DIM, (j + 1) * HEAD_DIM)
                if dil == 1:
                    out_ref[:, lanes] = t.astype(BF16)
                else:
                    fold_ref[j] = t
                    for r in range(dil):
                        rows = fold_ref[j, pl.ds(r, tm // dil, stride=dil), :]
                        out_ref[0, r, :, lanes] = rows.astype(BF16)

    for c in range(4):
        gate_ref[:, c * PROJ_CHUNK:(c + 1) * PROJ_CHUNK] = chunk(12 + c).astype(BF16)


def _project(x, g_pre, w_in, cosf, sinf):
    bsz, seq, _ = x.shape
    tm = PROJ_TM
    tiles = seq // tm
    n_tok = bsz * seq
    x2 = x.reshape(n_tok, D_MODEL)

    def row(i):
        return (i, 0)

    def folded(dil):
        return pl.BlockSpec((1, dil, tm // dil, B_W), lambda i: (i // tiles, 0, i % tiles, 0))

    out_shape = [jax.ShapeDtypeStruct((n_tok, A_Q), BF16),
                 jax.ShapeDtypeStruct((n_tok, A_KV), BF16),
                 jax.ShapeDtypeStruct((bsz, A_KV, seq), BF16)]
    out_specs = [pl.BlockSpec((tm, A_Q), row), pl.BlockSpec((tm, A_KV), row),
                 pl.BlockSpec((1, A_KV, tm), lambda i: (i // tiles, 0, i % tiles))]
    for dil in B_DILATIONS:
        for _ in range(3):
            if dil == 1:
                out_shape.append(jax.ShapeDtypeStruct((n_tok, B_W), BF16))
                out_specs.append(pl.BlockSpec((tm, B_W), row))
            else:
                out_shape.append(jax.ShapeDtypeStruct((bsz, dil, seq // dil, B_W), BF16))
                out_specs.append(folded(dil))
    out_shape.append(jax.ShapeDtypeStruct((n_tok, 2 * D_MODEL), BF16))
    out_specs.append(pl.BlockSpec((tm, 2 * D_MODEL), row))

    return pl.pallas_call(
        _proj_kernel,
        grid=(n_tok // tm,),
        in_specs=[pl.BlockSpec((tm, D_MODEL), row),
                  _resident((1, D_MODEL)),
                  _resident((D_MODEL, D_IN)),
                  pl.BlockSpec((tm, HEAD_DIM), lambda i: (i % tiles, 0)),
                  pl.BlockSpec((tm, HEAD_DIM), lambda i: (i % tiles, 0))],
        out_specs=out_specs,
        out_shape=out_shape,
        scratch_shapes=[pltpu.VMEM((B_HEADS, tm, HEAD_DIM), F32)],
        compiler_params=pltpu.CompilerParams(dimension_semantics=("arbitrary",),
                                             vmem_limit_bytes=VMEM_LIMIT),
        name="proj",
    )(x2, g_pre, w_in, cosf, sinf)


def _band_bias(n_keys, n_queries, half_window, step):
    rel = (jnp.arange(n_keys)[None, :, None] - jnp.arange(n_queries)[None, None, :]
           - step * jnp.arange(3)[:, None, None])
    return jnp.where(jnp.abs(rel) <= half_window, 0.0, NEG).astype(F32)


def _attn_a_kernel(sink_ref, bias_ref, q_ref, k_ref, vt_ref, o_ref):
    seq = k_ref.shape[1]
    step_q0 = pl.program_id(1) * (A_STEP_BLOCKS * A_QBLOCK)

    chunks = [(blk, j, [j * A_GROUP + c * A_CHUNK_HEADS + g for g in range(A_CHUNK_HEADS)])
              for blk in range(A_STEP_BLOCKS) for j in range(A_KV_HEADS)
              for c in range(A_GROUP // A_CHUNK_HEADS)]

    def window(blk):
        q0 = step_q0 + blk * A_QBLOCK
        kstart = pl.multiple_of(jnp.clip(q0 - A_QBLOCK, 0, seq - A_KBAND), A_QBLOCK)
        return q0, kstart

    def scores(blk, j, hs):
        _, kstart = window(blk)
        rows = slice(blk * A_QBLOCK, (blk + 1) * A_QBLOCK)
        kb = k_ref[0, pl.ds(kstart, A_KBAND), j * HEAD_DIM:(j + 1) * HEAD_DIM]
        qs = jnp.concatenate([q_ref[0, rows, h * HEAD_DIM:(h + 1) * HEAD_DIM] for h in hs],
                             axis=0)
        return lax.dot_general(kb, qs, (((1,), (1,)), ((), ())), preferred_element_type=F32)

    def softmax(st, blk, hs):
        q0, kstart = window(blk)
        bias = bias_ref[(q0 - kstart) // A_QBLOCK]
        st = st + jnp.concatenate([bias] * A_CHUNK_HEADS, axis=1)
        sk = jnp.concatenate([jnp.full((1, A_QBLOCK), sink_ref[h] * LOG2E, F32) for h in hs],
                             axis=1)
        m = jnp.maximum(jnp.max(st, axis=0, keepdims=True), sk)
        pt = jnp.exp2(st - m)
        den = jnp.sum(pt, axis=0, keepdims=True) + jnp.exp2(sk - m)
        return pt.astype(BF16), 1.0 / den

    def output(blk, j, hs, pt, rden):
        _, kstart = window(blk)
        rows = slice(blk * A_QBLOCK, (blk + 1) * A_QBLOCK)
        vtb = vt_ref[0, j * HEAD_DIM:(j + 1) * HEAD_DIM, pl.ds(kstart, A_KBAND)]
        ot = jnp.dot(vtb, pt, preferred_element_type=F32) * rden
        for g, h in enumerate(hs):
            o_ref[0, rows, h * HEAD_DIM:(h + 1) * HEAD_DIM] = (
                ot[:, g * A_QBLOCK:(g + 1) * A_QBLOCK].T.astype(BF16))

    n = len(chunks)
    st = {0: scores(*chunks[0])}
    sm = {}
    for c in range(n):
        if c + 1 < n:
            st[c + 1] = scores(*chunks[c + 1])
        sm[c] = softmax(st.pop(c), chunks[c][0], chunks[c][2])
        if c >= 1:
            output(*chunks[c - 1], *sm.pop(c - 1))
    output(*chunks[n - 1], *sm.pop(n - 1))


def _attention_a(sink, qa, ka, vat, bsz, seq):
    q3 = qa.reshape(bsz, seq, A_Q)
    k3 = ka.reshape(bsz, seq, A_KV)
    v3 = vat
    return pl.pallas_call(
        _attn_a_kernel,
        grid=(bsz, seq // (A_STEP_BLOCKS * A_QBLOCK)),
        in_specs=[pl.BlockSpec(memory_space=pltpu.SMEM),
                  _resident((3, A_KBAND, A_QBLOCK)),
                  pl.BlockSpec((1, A_STEP_BLOCKS * A_QBLOCK, A_Q), lambda b, i: (b, i, 0)),
                  pl.BlockSpec((1, seq, A_KV), lambda b, i: (b, 0, 0)),
                  pl.BlockSpec((1, A_KV, seq), lambda b, i: (b, 0, 0))],
        out_specs=pl.BlockSpec((1, A_STEP_BLOCKS * A_QBLOCK, A_Q), lambda b, i: (b, i, 0)),
        out_shape=jax.ShapeDtypeStruct((bsz, seq, A_Q), BF16),
        compiler_params=pltpu.CompilerParams(dimension_semantics=("arbitrary", "arbitrary"),
                                             vmem_limit_bytes=VMEM_LIMIT),
        name="attn_a",
    )(sink, _band_bias(A_KBAND, A_QBLOCK, A_HALF_WINDOW, A_QBLOCK), q3, k3, v3)


def _b_geometry(seq, dil):
    ld = seq // dil
    qb = min(B_QBLOCK, ld)
    kw = min(qb + 2 * B_HALF_WINDOW, ld)
    return ld, qb, kw, ld // qb


def _attn_b_kernel(bias0_ref, bias1_ref, bias2_ref,
                   q0_ref, k0_ref, v0_ref, q1_ref, k1_ref, v1_ref, q2_ref, k2_ref, v2_ref,
                   o_ref, acc_ref, lse_ref):
    seq = o_ref.shape[1]
    group_refs = ((bias0_ref, q0_ref, k0_ref, v0_ref), (bias1_ref, q1_ref, k1_ref, v1_ref),
                  (bias2_ref, q2_ref, k2_ref, v2_ref))

    for g, dil in enumerate(B_DILATIONS):
        bias_ref, q_ref, k_ref, v_ref = group_refs[g]
        ld, qb, kw, nq = _b_geometry(seq, dil)

        n_blocks = dil * nq
        per_iter = min(B_UNROLL, n_blocks)

        def window(t, qb=qb, kw=kw, nq=nq, ld=ld):
            r = t // nq
            q0 = pl.multiple_of((t % nq) * qb, qb)
            kstart = pl.multiple_of(jnp.clip(q0 - B_HALF_WINDOW, 0, ld - kw), B_HALF_WINDOW)
            return r, q0, kstart

        def scores(t, window=window, qb=qb, kw=kw, q_ref=q_ref, k_ref=k_ref):
            r, q0, kstart = window(t)
            q = q_ref[0, r, pl.ds(q0, qb), :]
            kb = k_ref[0, r, pl.ds(kstart, kw), :]
            return lax.dot_general(kb, q, (((1,), (1,)), ((), ())), preferred_element_type=F32)

        def softmax(st, t, window=window, bias_ref=bias_ref):
            _, q0, kstart = window(t)
            st = st + bias_ref[(q0 - kstart) // B_HALF_WINDOW]
            m = jnp.max(st, axis=0, keepdims=True)
            pt = jnp.exp2(st - m)
            den = jnp.sum(pt, axis=0, keepdims=True)
            return pt.astype(BF16), 1.0 / den, m + jnp.log2(den)

        def output(t, pt, rden, lse2, window=window, dil=dil, qb=qb, kw=kw, v_ref=v_ref, g=g):
            r, q0, kstart = window(t)
            vb = v_ref[0, r, pl.ds(kstart, kw), :]
            ot = lax.dot_general(vb, pt, (((0,), (0,)), ((), ())),
                                 preferred_element_type=F32) * rden
            if dil == 1:
                rows = pl.ds(q0, qb)
            else:
                rows = pl.ds(q0 * dil + r, qb, stride=dil)
            acc_ref[g, rows, :] = ot.T
            lse_ref[g, rows, :] = jnp.broadcast_to(lse2, (HEAD_DIM, qb)).T

        def blocks(it, carry, per_iter=per_iter, scores=scores, softmax=softmax, output=output):
            ts = [it * per_iter + u for u in range(per_iter)]
            st = {0: scores(ts[0])}
            sm = {}
            for u in range(per_iter):
                if u + 1 < per_iter:
                    st[u + 1] = scores(ts[u + 1])
                sm[u] = softmax(st.pop(u), ts[u])
                if u >= 1:
                    output(ts[u - 1], *sm.pop(u - 1))
            output(ts[per_iter - 1], *sm.pop(per_iter - 1))
            return carry

        if n_blocks == per_iter:
            blocks(0, 0)
        else:
            lax.fori_loop(0, n_blocks // per_iter, blocks, 0)

    def merge(t, carry):
        rows = pl.ds(pl.multiple_of(t * MERGE_ROWS, MERGE_ROWS), MERGE_ROWS)
        ls = [lse_ref[g, rows, :] for g in range(N_B)]
        top = jnp.maximum(jnp.maximum(ls[0], ls[1]), ls[2])
        es = [jnp.exp2(l - top) for l in ls]
        rtot = 1.0 / (es[0] + es[1] + es[2])
        out = sum(es[g] * acc_ref[g, rows, :] for g in range(N_B)) * rtot
        o_ref[0, rows, :] = out.astype(BF16)
        return carry

    lax.fori_loop(0, seq // MERGE_ROWS, merge, 0)


def _attention_b(qkv, bsz, seq):
    args, in_specs = [], []
    for dil in B_DILATIONS:
        _, qb, kw, _ = _b_geometry(seq, dil)
        args.append(_band_bias(kw, qb, B_HALF_WINDOW, B_HALF_WINDOW))
        in_specs.append(_resident((3, kw, qb)))
    for g, dil in enumerate(B_DILATIONS):
        ld = seq // dil
        for part in range(3):
            args.append(qkv[3 * g + part].reshape(bsz, dil, ld, B_W))
            in_specs.append(pl.BlockSpec((1, dil, ld, HEAD_DIM), lambda b, h: (b, 0, 0, h)))
    return pl.pallas_call(
        _attn_b_kernel,
        grid=(bsz, B_HEADS),
        in_specs=in_specs,
        out_specs=pl.BlockSpec((1, seq, HEAD_DIM), lambda b, h: (b, 0, h)),
        out_shape=jax.ShapeDtypeStruct((bsz, seq, B_W), BF16),
        scratch_shapes=[pltpu.VMEM((N_B, seq, HEAD_DIM), F32),
                        pltpu.VMEM((N_B, seq, HEAD_DIM), F32)],
        compiler_params=pltpu.CompilerParams(dimension_semantics=("arbitrary", "arbitrary"),
                                             vmem_limit_bytes=VMEM_LIMIT),
        name="attn_b",
    )(*args)


def _post_kernel(x_ref, oa_ref, ob_ref, gate_ref, wa_ref, wb_ref, wo_ref, w1_ref, w2_ref,
                 g_mix_ref, g_pre_ref, g_post_ref, y_ref):
    ya = jnp.dot(oa_ref[...], wa_ref[...], preferred_element_type=F32)
    yb = jnp.dot(ob_ref[...], wb_ref[...], preferred_element_type=F32)
    ga = gate_ref[:, :D_MODEL].astype(F32)
    gb = gate_ref[:, D_MODEL:].astype(F32)
    mix_in = jax.nn.sigmoid(ga) * ya + jax.nn.sigmoid(gb) * yb
    mix = jnp.dot(mix_in.astype(BF16), wo_ref[...], preferred_element_type=F32)
    x1 = x_ref[...] + _rms(mix, g_mix_ref[...])
    h2 = _rms(x1, g_pre_ref[...]).astype(BF16)
    acc = jnp.zeros_like(x1)
    ff_chunk = D_MODEL
    for c in range(D_FF // ff_chunk):
        cols = slice(c * ff_chunk, (c + 1) * ff_chunk)
        u = jnp.maximum(jnp.dot(h2, w1_ref[:, cols], preferred_element_type=F32), 0.0)
        acc = acc + jnp.dot((u * u).astype(BF16), w2_ref[cols, :], preferred_element_type=F32)
    y_ref[...] = x1 + _rms(acc, g_post_ref[...])


def _post(x, oa, ob, gates, w_a, w_b, w_o, w_1, w_2, g_mix, g_pre, g_post):
    bsz, seq, _ = x.shape
    n_tok = bsz * seq
    tm = POST_TM

    def row(i):
        return (i, 0)

    y = pl.pallas_call(
        _post_kernel,
        grid=(n_tok // tm,),
        in_specs=[pl.BlockSpec((tm, D_MODEL), row),
                  pl.BlockSpec((tm, A_Q), row),
                  pl.BlockSpec((tm, B_W), row),
                  pl.BlockSpec((tm, 2 * D_MODEL), row),
                  _resident((A_Q, D_MODEL)), _resident((B_W, D_MODEL)),
                  _resident((D_MODEL, D_MODEL)),
                  _resident((D_MODEL, D_FF)), _resident((D_FF, D_MODEL)),
                  _resident((1, D_MODEL)), _resident((1, D_MODEL)), _resident((1, D_MODEL))],
        out_specs=pl.BlockSpec((tm, D_MODEL), row),
        out_shape=jax.ShapeDtypeStruct((n_tok, D_MODEL), F32),
        compiler_params=pltpu.CompilerParams(dimension_semantics=("arbitrary",),
                                             vmem_limit_bytes=VMEM_LIMIT),
        name="post",
    )(x.reshape(n_tok, D_MODEL), oa.reshape(n_tok, A_Q), ob.reshape(n_tok, B_W), gates,
      w_a, w_b, w_o, w_1, w_2, g_mix, g_pre, g_post)
    return y.reshape(bsz, seq, D_MODEL)


def _rope_tables(seq):
    half = HEAD_DIM // 2
    inv = jnp.power(ROPE_THETA, -jnp.arange(half, dtype=F32) / half)
    ang = jnp.arange(seq).astype(F32)[:, None] * inv[None, :]
    cos, sin = jnp.cos(ang), jnp.sin(ang)
    return jnp.concatenate([cos, cos], axis=-1), jnp.concatenate([-sin, sin], axis=-1)


def _layer(x, w_in, sink, w_a, w_b, w_o, g_pre_mix, g_post_mix, g_pre_mlp, g_post_mlp, w_1, w_2):
    bsz, seq, _ = x.shape
    cosf, sinf = _rope_tables(seq)
    outs = _project(x, g_pre_mix.reshape(1, D_MODEL), w_in, cosf, sinf)
    qa, ka, va = outs[:3]
    qkv_b = outs[3:12]
    gates = outs[12]
    oa = _attention_a(sink, qa, ka, va, bsz, seq)
    ob = _attention_b(qkv_b, bsz, seq)
    return _post(x, oa, ob, gates, w_a, w_b, w_o, w_1, w_2,
                 g_post_mix.reshape(1, D_MODEL), g_pre_mlp.reshape(1, D_MODEL),
                 g_post_mlp.reshape(1, D_MODEL))


def kernel(x_prompt, x_sample, w_in, sink, w_a, w_b, w_o, g_pre_mix, g_post_mix, g_pre_mlp,
           g_post_mlp, w_1, w_2):
    depth = w_in.shape[0]
    xs = [x_prompt, x_sample]
    for l in range(depth):
        ws = (w_in[l].astype(BF16), sink[l], w_a[l].astype(BF16), w_b[l].astype(BF16),
              w_o[l].astype(BF16), g_pre_mix[l], g_post_mix[l], g_pre_mlp[l], g_post_mlp[l],
              w_1[l].astype(BF16), w_2[l].astype(BF16))
        xs = [_layer(x, *ws) for x in xs]
    return (xs[0], xs[1])
```

```python
import functools

import jax
import jax.numpy as jnp
from jax import lax
from jax.experimental import pallas as pl
from jax.experimental.pallas import tpu as pltpu

D_MODEL = 1024
HEAD_DIM = 128
A_Q_HEADS = 8
A_KV_HEADS = 2
A_GROUP = A_Q_HEADS // A_KV_HEADS
A_HALF_WINDOW = 128
B_DILATIONS = (1, 4, 16)
FOLD_STRIDE = 4
B_HEADS = 4
B_HALF_WINDOW = 64
D_FF = 4 * D_MODEL
ROPE_THETA = 10000.0
EPS = 1e-6
NEG = -1e30

A_Q = A_Q_HEADS * HEAD_DIM
A_KV = A_KV_HEADS * HEAD_DIM
B_W = B_HEADS * HEAD_DIM
N_B = len(B_DILATIONS)
D_IN = A_Q + 2 * A_KV + 3 * N_B * B_W + 2 * D_MODEL

LOG2E = 1.4426950408889634
LN2 = 0.6931471805599453
Q_SCALE = HEAD_DIM ** -0.5 * LOG2E

BF16 = jnp.bfloat16
F32 = jnp.float32

PROJ_TM = 512
POST_TM = 512
A_QBLOCK = 128
A_KBAND = 3 * A_QBLOCK
A_CHUNK_HEADS = 2
A_STEP_BLOCKS = 4
B_QBLOCK = 256
MERGE_ROWS = 128
B_UNROLL = 8
PROJ_CHUNK = 512
VMEM_LIMIT = 56 * 1024 * 1024


def _resident(shape):
    nd = len(shape)
    return pl.BlockSpec(shape, lambda *_: (0,) * nd, pipeline_mode=pl.Buffered(1))


def _rms(x, g):
    return x * lax.rsqrt(jnp.mean(x * x, axis=-1, keepdims=True) + EPS) * g


def _proj_kernel(x_ref, g_ref, w_ref, cos_ref, sin_ref,
                 qa_ref, ka_ref, vat_ref,
                 q0_ref, k0_ref, v0_ref, q1_ref, k1_ref, v1_ref, q2_ref, k2_ref, v2_ref,
                 gate_ref, fold_ref, fold2_ref):
    tm = x_ref.shape[0]
    h = _rms(x_ref[...], g_ref[...]).astype(BF16)
    cosf = cos_ref[...]
    sinf = sin_ref[...]
    scale = Q_SCALE

    def rope(t):
        return t * cosf + pltpu.roll(t, HEAD_DIM // 2, 1) * sinf

    def chunk(c):
        return jnp.dot(h, w_ref[:, c * PROJ_CHUNK:(c + 1) * PROJ_CHUNK],
                       preferred_element_type=F32)

    def heads(z):
        return [z[:, j * HEAD_DIM:(j + 1) * HEAD_DIM] for j in range(PROJ_CHUNK // HEAD_DIM)]

    for c in range(2):
        for j, t in enumerate(heads(chunk(c))):
            col = (c * 4 + j) * HEAD_DIM
            qa_ref[:, col:col + HEAD_DIM] = (rope(t) * scale).astype(BF16)
    zkv = heads(chunk(2))
    for j in range(A_KV_HEADS):
        ka_ref[:, j * HEAD_DIM:(j + 1) * HEAD_DIM] = rope(zkv[j]).astype(BF16)
        vat_ref[0, j * HEAD_DIM:(j + 1) * HEAD_DIM, :] = zkv[A_KV_HEADS + j].T.astype(BF16)

    group_refs = ((q0_ref, k0_ref, v0_ref), (q1_ref, k1_ref, v1_ref), (q2_ref, k2_ref, v2_ref))
    for g, dil in enumerate(B_DILATIONS):
        for part in range(3):
            out_ref = group_refs[g][part]
            for j, t in enumerate(heads(chunk(3 + 3 * g + part))):
                if part == 0:
                    t = rope(t) * scale
                elif part == 1:
                    t = rope(t)
                lanes = slice(j * HEAD_DIM, (j + 1) * HEAD_DIM)
                if dil == 1:
                    out_ref[:, lanes] = t.astype(BF16)
                elif dil == FOLD_STRIDE:
                    fold_ref[j] = t
                    for r in range(dil):
                        rows = fold_ref[j, pl.ds(r, tm // dil, stride=dil), :]
                        out_ref[0, r, :, lanes] = rows.astype(BF16)
                else:
                    fold_ref[j] = t
                    for b in range(FOLD_STRIDE):
                        fold2_ref[j, b] = fold_ref[j, pl.ds(b, tm // FOLD_STRIDE, stride=FOLD_STRIDE), :]
                        for a in range(dil // FOLD_STRIDE):
                            rows = fold2_ref[j, b, pl.ds(a, tm // dil, stride=FOLD_STRIDE), :]
                            out_ref[0, FOLD_STRIDE * a + b, :, lanes] = rows.astype(BF16)

    for c in range(4):
        gate_ref[:, c * PROJ_CHUNK:(c + 1) * PROJ_CHUNK] = chunk(12 + c).astype(BF16)


def _project(x, g_pre, w_in, cosf, sinf):
    bsz, seq, _ = x.shape
    tm = PROJ_TM
    tiles = seq // tm
    n_tok = bsz * seq
    x2 = x.reshape(n_tok, D_MODEL)

    def row(i):
        return (i, 0)

    def folded(dil):
        return pl.BlockSpec((1, dil, tm // dil, B_W), lambda i: (i // tiles, 0, i % tiles, 0))

    out_shape = [jax.ShapeDtypeStruct((n_tok, A_Q), BF16),
                 jax.ShapeDtypeStruct((n_tok, A_KV), BF16),
                 jax.ShapeDtypeStruct((bsz, A_KV, seq), BF16)]
    out_specs = [pl.BlockSpec((tm, A_Q), row), pl.BlockSpec((tm, A_KV), row),
                 pl.BlockSpec((1, A_KV, tm), lambda i: (i // tiles, 0, i % tiles))]
    for dil in B_DILATIONS:
        for _ in range(3):
            if dil == 1:
                out_shape.append(jax.ShapeDtypeStruct((n_tok, B_W), BF16))
                out_specs.append(pl.BlockSpec((tm, B_W), row))
            else:
                out_shape.append(jax.ShapeDtypeStruct((bsz, dil, seq // dil, B_W), BF16))
                out_specs.append(folded(dil))
    out_shape.append(jax.ShapeDtypeStruct((n_tok, 2 * D_MODEL), BF16))
    out_specs.append(pl.BlockSpec((tm, 2 * D_MODEL), row))

    return pl.pallas_call(
        _proj_kernel,
        grid=(n_tok // tm,),
        in_specs=[pl.BlockSpec((tm, D_MODEL), row),
                  _resident((1, D_MODEL)),
                  _resident((D_MODEL, D_IN)),
                  pl.BlockSpec((tm, HEAD_DIM), lambda i: (i % tiles, 0)),
                  pl.BlockSpec((tm, HEAD_DIM), lambda i: (i % tiles, 0))],
        out_specs=out_specs,
        out_shape=out_shape,
        scratch_shapes=[pltpu.VMEM((B_HEADS, tm, HEAD_DIM), F32),
                        pltpu.VMEM((B_HEADS, FOLD_STRIDE, tm // FOLD_STRIDE, HEAD_DIM), F32)],
        compiler_params=pltpu.CompilerParams(dimension_semantics=("arbitrary",),
                                             vmem_limit_bytes=VMEM_LIMIT),
        name="proj",
    )(x2, g_pre, w_in, cosf, sinf)


def _band_bias(n_keys, n_queries, half_window, step):
    rel = (jnp.arange(n_keys)[None, :, None] - jnp.arange(n_queries)[None, None, :]
           - step * jnp.arange(3)[:, None, None])
    return jnp.where(jnp.abs(rel) <= half_window, 0.0, NEG).astype(F32)


def _attn_a_kernel(sink_ref, bias_ref, q_ref, k_ref, vt_ref, o_ref):
    seq = k_ref.shape[1]
    step_q0 = pl.program_id(1) * (A_STEP_BLOCKS * A_QBLOCK)

    chunks = [(blk, j, [j * A_GROUP + c * A_CHUNK_HEADS + g for g in range(A_CHUNK_HEADS)])
              for blk in range(A_STEP_BLOCKS) for j in range(A_KV_HEADS)
              for c in range(A_GROUP // A_CHUNK_HEADS)]

    def window(blk):
        q0 = step_q0 + blk * A_QBLOCK
        kstart = pl.multiple_of(jnp.clip(q0 - A_QBLOCK, 0, seq - A_KBAND), A_QBLOCK)
        return q0, kstart

    def scores(blk, j, hs):
        _, kstart = window(blk)
        rows = slice(blk * A_QBLOCK, (blk + 1) * A_QBLOCK)
        kb = k_ref[0, pl.ds(kstart, A_KBAND), j * HEAD_DIM:(j + 1) * HEAD_DIM]
        qs = jnp.concatenate([q_ref[0, rows, h * HEAD_DIM:(h + 1) * HEAD_DIM] for h in hs],
                             axis=0)
        return lax.dot_general(kb, qs, (((1,), (1,)), ((), ())), preferred_element_type=F32)

    def softmax(st, blk, hs):
        q0, kstart = window(blk)
        bias = bias_ref[(q0 - kstart) // A_QBLOCK]
        st = st + jnp.concatenate([bias] * A_CHUNK_HEADS, axis=1)
        sk = jnp.concatenate([jnp.full((1, A_QBLOCK), sink_ref[h] * LOG2E, F32) for h in hs],
                             axis=1)
        m = jnp.maximum(jnp.max(st, axis=0, keepdims=True), sk)
        pt = jnp.exp2(st - m)
        den = jnp.sum(pt, axis=0, keepdims=True) + jnp.exp2(sk - m)
        return pt.astype(BF16), 1.0 / den

    def output(blk, j, hs, pt, rden):
        _, kstart = window(blk)
        rows = slice(blk * A_QBLOCK, (blk + 1) * A_QBLOCK)
        vtb = vt_ref[0, j * HEAD_DIM:(j + 1) * HEAD_DIM, pl.ds(kstart, A_KBAND)]
        ot = jnp.dot(vtb, pt, preferred_element_type=F32) * rden
        for g, h in enumerate(hs):
            o_ref[0, rows, h * HEAD_DIM:(h + 1) * HEAD_DIM] = (
                ot[:, g * A_QBLOCK:(g + 1) * A_QBLOCK].T.astype(BF16))

    n = len(chunks)
    st = {0: scores(*chunks[0])}
    sm = {}
    for c in range(n):
        if c + 1 < n:
            st[c + 1] = scores(*chunks[c + 1])
        sm[c] = softmax(st.pop(c), chunks[c][0], chunks[c][2])
        if c >= 1:
            output(*chunks[c - 1], *sm.pop(c - 1))
    output(*chunks[n - 1], *sm.pop(n - 1))


def _attention_a(sink, qa, ka, vat, bsz, seq):
    q3 = qa.reshape(bsz, seq, A_Q)
    k3 = ka.reshape(bsz, seq, A_KV)
    v3 = vat
    return pl.pallas_call(
        _attn_a_kernel,
        grid=(bsz, seq // (A_STEP_BLOCKS * A_QBLOCK)),
        in_specs=[pl.BlockSpec(memory_space=pltpu.SMEM),
                  _resident((3, A_KBAND, A_QBLOCK)),
                  pl.BlockSpec((1, A_STEP_BLOCKS * A_QBLOCK, A_Q), lambda b, i: (b, i, 0)),
                  pl.BlockSpec((1, seq, A_KV), lambda b, i: (b, 0, 0)),
                  pl.BlockSpec((1, A_KV, seq), lambda b, i: (b, 0, 0))],
        out_specs=pl.BlockSpec((1, A_STEP_BLOCKS * A_QBLOCK, A_Q), lambda b, i: (b, i, 0)),
        out_shape=jax.ShapeDtypeStruct((bsz, seq, A_Q), BF16),
        compiler_params=pltpu.CompilerParams(dimension_semantics=("arbitrary", "arbitrary"),
                                             vmem_limit_bytes=VMEM_LIMIT),
        name="attn_a",
    )(sink, _band_bias(A_KBAND, A_QBLOCK, A_HALF_WINDOW, A_QBLOCK), q3, k3, v3)


def _b_geometry(seq, dil):
    ld = seq // dil
    qb = min(B_QBLOCK, ld)
    kw = min(qb + 2 * B_HALF_WINDOW, ld)
    return ld, qb, kw, ld // qb


def _attn_b_kernel(bias0_ref, bias1_ref, bias2_ref,
                   q0_ref, k0_ref, v0_ref, q1_ref, k1_ref, v1_ref, q2_ref, k2_ref, v2_ref,
                   o_ref, acc_ref, lse_ref):
    seq = o_ref.shape[1]
    group_refs = ((bias0_ref, q0_ref, k0_ref, v0_ref), (bias1_ref, q1_ref, k1_ref, v1_ref),
                  (bias2_ref, q2_ref, k2_ref, v2_ref))

    for g, dil in enumerate(B_DILATIONS):
        bias_ref, q_ref, k_ref, v_ref = group_refs[g]
        ld, qb, kw, nq = _b_geometry(seq, dil)

        n_blocks = dil * nq
        per_iter = min(B_UNROLL, n_blocks)

        def window(t, qb=qb, kw=kw, nq=nq, ld=ld):
            r = t // nq
            q0 = pl.multiple_of((t % nq) * qb, qb)
            kstart = pl.multiple_of(jnp.clip(q0 - B_HALF_WINDOW, 0, ld - kw), B_HALF_WINDOW)
            return r, q0, kstart

        def scores(t, window=window, qb=qb, kw=kw, q_ref=q_ref, k_ref=k_ref):
            r, q0, kstart = window(t)
            q = q_ref[0, r, pl.ds(q0, qb), :]
            kb = k_ref[0, r, pl.ds(kstart, kw), :]
            return lax.dot_general(kb, q, (((1,), (1,)), ((), ())), preferred_element_type=F32)

        def softmax(st, t, window=window, bias_ref=bias_ref):
            _, q0, kstart = window(t)
            st = st + bias_ref[(q0 - kstart) // B_HALF_WINDOW]
            m = jnp.max(st, axis=0, keepdims=True)
            pt = jnp.exp2(st - m)
            den = jnp.sum(pt, axis=0, keepdims=True)
            return pt.astype(BF16), 1.0 / den, m + jnp.log2(den)

        def output(t, pt, rden, lse2, window=window, dil=dil, qb=qb, kw=kw, v_ref=v_ref, g=g):
            r, q0, kstart = window(t)
            vb = v_ref[0, r, pl.ds(kstart, kw), :]
            ot = lax.dot_general(vb, pt, (((0,), (0,)), ((), ())),
                                 preferred_element_type=F32) * rden
            if dil == 1:
                rows = pl.ds(q0, qb)
            else:
                rows = pl.ds(q0 * dil + r, qb, stride=dil)
            acc_ref[g, rows, :] = ot.T
            lse_ref[g, rows, :] = jnp.broadcast_to(lse2, (HEAD_DIM, qb)).T

        def blocks(it, carry, per_iter=per_iter, scores=scores, softmax=softmax, output=output):
            ts = [it * per_iter + u for u in range(per_iter)]
            st = {0: scores(ts[0])}
            sm = {}
            for u in range(per_iter):
                if u + 1 < per_iter:
                    st[u + 1] = scores(ts[u + 1])
                sm[u] = softmax(st.pop(u), ts[u])
                if u >= 1:
                    output(ts[u - 1], *sm.pop(u - 1))
            output(ts[per_iter - 1], *sm.pop(per_iter - 1))
            return carry

        if n_blocks == per_iter:
            blocks(0, 0)
        else:
            lax.fori_loop(0, n_blocks // per_iter, blocks, 0)

    def merge(t, carry):
        rows = pl.ds(pl.multiple_of(t * MERGE_ROWS, MERGE_ROWS), MERGE_ROWS)
        ls = [lse_ref[g, rows, :] for g in range(N_B)]
        top = jnp.maximum(jnp.maximum(ls[0], ls[1]), ls[2])
        es = [jnp.exp2(l - top) for l in ls]
        rtot = 1.0 / (es[0] + es[1] + es[2])
        out = sum(es[g] * acc_ref[g, rows, :] for g in range(N_B)) * rtot
        o_ref[0, rows, :] = out.astype(BF16)
        return carry

    lax.fori_loop(0, seq // MERGE_ROWS, merge, 0)


def _attention_b(qkv, bsz, seq):
    args, in_specs = [], []
    for dil in B_DILATIONS:
        _, qb, kw, _ = _b_geometry(seq, dil)
        args.append(_band_bias(kw, qb, B_HALF_WINDOW, B_HALF_WINDOW))
        in_specs.append(_resident((3, kw, qb)))
    for g, dil in enumerate(B_DILATIONS):
        ld = seq // dil
        for part in range(3):
            args.append(qkv[3 * g + part].reshape(bsz, dil, ld, B_W))
            in_specs.append(pl.BlockSpec((1, dil, ld, HEAD_DIM), lambda b, h: (b, 0, 0, h)))
    return pl.pallas_call(
        _attn_b_kernel,
        grid=(bsz, B_HEADS),
        in_specs=in_specs,
        out_specs=pl.BlockSpec((1, seq, HEAD_DIM), lambda b, h: (b, 0, h)),
        out_shape=jax.ShapeDtypeStruct((bsz, seq, B_W), BF16),
        scratch_shapes=[pltpu.VMEM((N_B, seq, HEAD_DIM), F32),
                        pltpu.VMEM((N_B, seq, HEAD_DIM), F32)],
        compiler_params=pltpu.CompilerParams(dimension_semantics=("arbitrary", "arbitrary"),
                                             vmem_limit_bytes=VMEM_LIMIT),
        name="attn_b",
    )(*args)


def _post_kernel(x_ref, oa_ref, ob_ref, gate_ref, wa_ref, wb_ref, wo_ref, w1_ref, w2_ref,
                 g_mix_ref, g_pre_ref, g_post_ref, y_ref):
    ya = jnp.dot(oa_ref[...], wa_ref[...], preferred_element_type=F32)
    yb = jnp.dot(ob_ref[...], wb_ref[...], preferred_element_type=F32)
    ga = gate_ref[:, :D_MODEL].astype(F32)
    gb = gate_ref[:, D_MODEL:].astype(F32)
    mix_in = jax.nn.sigmoid(ga) * ya + jax.nn.sigmoid(gb) * yb
    mix = jnp.dot(mix_in.astype(BF16), wo_ref[...], preferred_element_type=F32)
    x1 = x_ref[...] + _rms(mix, g_mix_ref[...])
    h2 = _rms(x1, g_pre_ref[...]).astype(BF16)
    acc = jnp.zeros_like(x1)
    ff_chunk = D_MODEL
    for c in range(D_FF // ff_chunk):
        cols = slice(c * ff_chunk, (c + 1) * ff_chunk)
        u = jnp.maximum(jnp.dot(h2, w1_ref[:, cols], preferred_element_type=F32), 0.0)
        acc = acc + jnp.dot((u * u).astype(BF16), w2_ref[cols, :], preferred_element_type=F32)
    y_ref[...] = x1 + _rms(acc, g_post_ref[...])


def _post(x, oa, ob, gates, w_a, w_b, w_o, w_1, w_2, g_mix, g_pre, g_post):
    bsz, seq, _ = x.shape
    n_tok = bsz * seq
    tm = POST_TM

    def row(i):
        return (i, 0)

    y = pl.pallas_call(
        _post_kernel,
        grid=(n_tok // tm,),
        in_specs=[pl.BlockSpec((tm, D_MODEL), row),
                  pl.BlockSpec((tm, A_Q), row),
                  pl.BlockSpec((tm, B_W), row),
                  pl.BlockSpec((tm, 2 * D_MODEL), row),
                  _resident((A_Q, D_MODEL)), _resident((B_W, D_MODEL)),
                  _resident((D_MODEL, D_MODEL)),
                  _resident((D_MODEL, D_FF)), _resident((D_FF, D_MODEL)),
                  _resident((1, D_MODEL)), _resident((1, D_MODEL)), _resident((1, D_MODEL))],
        out_specs=pl.BlockSpec((tm, D_MODEL), row),
        out_shape=jax.ShapeDtypeStruct((n_tok, D_MODEL), F32),
        compiler_params=pltpu.CompilerParams(dimension_semantics=("arbitrary",),
                                             vmem_limit_bytes=VMEM_LIMIT),
        name="post",
    )(x.reshape(n_tok, D_MODEL), oa.reshape(n_tok, A_Q), ob.reshape(n_tok, B_W), gates,
      w_a, w_b, w_o, w_1, w_2, g_mix, g_pre, g_post)
    return y.reshape(bsz, seq, D_MODEL)


def _rope_tables(seq):
    half = HEAD_DIM // 2
    inv = jnp.power(ROPE_THETA, -jnp.arange(half, dtype=F32) / half)
    ang = jnp.arange(seq).astype(F32)[:, None] * inv[None, :]
    cos, sin = jnp.cos(ang), jnp.sin(ang)
    return jnp.concatenate([cos, cos], axis=-1), jnp.concatenate([-sin, sin], axis=-1)


def _layer(x, w_in, sink, w_a, w_b, w_o, g_pre_mix, g_post_mix, g_pre_mlp, g_post_mlp, w_1, w_2):
    bsz, seq, _ = x.shape
    cosf, sinf = _rope_tables(seq)
    outs = _project(x, g_pre_mix.reshape(1, D_MODEL), w_in, cosf, sinf)
    qa, ka, va = outs[:3]
    qkv_b = outs[3:12]
    gates = outs[12]
    oa = _attention_a(sink, qa, ka, va, bsz, seq)
    ob = _attention_b(qkv_b, bsz, seq)
    return _post(x, oa, ob, gates, w_a, w_b, w_o, w_1, w_2,
                 g_post_mix.reshape(1, D_MODEL), g_pre_mlp.reshape(1, D_MODEL),
                 g_post_mlp.reshape(1, D_MODEL))


def kernel(x_prompt, x_sample, w_in, sink, w_a, w_b, w_o, g_pre_mix, g_post_mix, g_pre_mlp,
           g_post_mlp, w_1, w_2):
    depth = w_in.shape[0]
    xs = [x_prompt, x_sample]
    for l in range(depth):
        ws = (w_in[l].astype(BF16), sink[l], w_a[l].astype(BF16), w_b[l].astype(BF16),
              w_o[l].astype(BF16), g_pre_mix[l], g_post_mix[l], g_pre_mlp[l], g_post_mlp[l],
              w_1[l].astype(BF16), w_2[l].astype(BF16))
        xs = [_layer(x, *ws) for x in xs]
    return (xs[0], xs[1])
```

```python
import functools

import jax
import jax.numpy as jnp
from jax import lax
from jax.experimental import pallas as pl
from jax.experimental.pallas import tpu as pltpu

D_MODEL = 1024
HEAD_DIM = 128
A_Q_HEADS = 8
A_KV_HEADS = 2
A_GROUP = A_Q_HEADS // A_KV_HEADS
A_HALF_WINDOW = 128
B_DILATIONS = (1, 4, 16)
FOLD_STRIDE = 4
B_HEADS = 4
B_HALF_WINDOW = 64
D_FF = 4 * D_MODEL
ROPE_THETA = 10000.0
EPS = 1e-6
NEG = -1e30

A_Q = A_Q_HEADS * HEAD_DIM
A_KV = A_KV_HEADS * HEAD_DIM
B_W = B_HEADS * HEAD_DIM
N_B = len(B_DILATIONS)
D_IN = A_Q + 2 * A_KV + 3 * N_B * B_W + 2 * D_MODEL

LOG2E = 1.4426950408889634
LN2 = 0.6931471805599453
Q_SCALE = HEAD_DIM ** -0.5 * LOG2E

BF16 = jnp.bfloat16
F32 = jnp.float32

PROJ_TM = 512
POST_TM = 512
A_QBLOCK = 128
A_KBAND = 3 * A_QBLOCK
A_CHUNK_HEADS = 2
A_STEP_BLOCKS = 4
B_QBLOCK = 256
MERGE_ROWS = 128
B_UNROLL = 16
PROJ_CHUNK = 512
VMEM_LIMIT = 56 * 1024 * 1024


def _resident(shape):
    nd = len(shape)
    return pl.BlockSpec(shape, lambda *_: (0,) * nd, pipeline_mode=pl.Buffered(1))


def _rms(x, g):
    return x * lax.rsqrt(jnp.mean(x * x, axis=-1, keepdims=True) + EPS) * g


def _proj_kernel(x_ref, g_ref, w_ref, cos_ref, sin_ref,
                 qa_ref, ka_ref, vat_ref,
                 q0_ref, k0_ref, v0_ref, q1_ref, k1_ref, v1_ref, q2_ref, k2_ref, v2_ref,
                 gate_ref, fold_ref, fold2_ref):
    tm = x_ref.shape[0]
    h = _rms(x_ref[...], g_ref[...]).astype(BF16)
    cosf = cos_ref[...]
    sinf = sin_ref[...]
    scale = Q_SCALE

    def rope(t):
        return t * cosf + pltpu.roll(t, HEAD_DIM // 2, 1) * sinf

    def chunk(c):
        return jnp.dot(h, w_ref[:, c * PROJ_CHUNK:(c + 1) * PROJ_CHUNK],
                       preferred_element_type=F32)

    def heads(z):
        return [z[:, j * HEAD_DIM:(j + 1) * HEAD_DIM] for j in range(PROJ_CHUNK // HEAD_DIM)]

    for c in range(2):
        for j, t in enumerate(heads(chunk(c))):
            col = (c * 4 + j) * HEAD_DIM
            qa_ref[:, col:col + HEAD_DIM] = (rope(t) * scale).astype(BF16)
    zkv = heads(chunk(2))
    for j in range(A_KV_HEADS):
        ka_ref[:, j * HEAD_DIM:(j + 1) * HEAD_DIM] = rope(zkv[j]).astype(BF16)
        vat_ref[0, j * HEAD_DIM:(j + 1) * HEAD_DIM, :] = zkv[A_KV_HEADS + j].T.astype(BF16)

    group_refs = ((q0_ref, k0_ref, v0_ref), (q1_ref, k1_ref, v1_ref), (q2_ref, k2_ref, v2_ref))
    for g, dil in enumerate(B_DILATIONS):
        for part in range(3):
            out_ref = group_refs[g][part]
            for j, t in enumerate(heads(chunk(3 + 3 * g + part))):
                if part == 0:
                    t = rope(t) * scale
                elif part == 1:
                    t = rope(t)
                lanes = slice(j * HEAD_DIM, (j + 1) * HEAD_DIM)
                if dil == 1:
                    out_ref[:, lanes] = t.astype(BF16)
                elif dil == FOLD_STRIDE:
                    fold_ref[j] = t
                    for r in range(dil):
                        rows = fold_ref[j, pl.ds(r, tm // dil, stride=dil), :]
                        out_ref[0, r, :, lanes] = rows.astype(BF16)
                else:
                    fold_ref[j] = t
                    for b in range(FOLD_STRIDE):
                        fold2_ref[j, b] = fold_ref[j, pl.ds(b, tm // FOLD_STRIDE, stride=FOLD_STRIDE), :]
                        for a in range(dil // FOLD_STRIDE):
                            rows = fold2_ref[j, b, pl.ds(a, tm // dil, stride=FOLD_STRIDE), :]
                            out_ref[0, FOLD_STRIDE * a + b, :, lanes] = rows.astype(BF16)

    for c in range(4):
        gate_ref[:, c * PROJ_CHUNK:(c + 1) * PROJ_CHUNK] = chunk(12 + c).astype(BF16)


def _project(x, g_pre, w_in, cosf, sinf):
    bsz, seq, _ = x.shape
    tm = PROJ_TM
    tiles = seq // tm
    n_tok = bsz * seq
    x2 = x.reshape(n_tok, D_MODEL)

    def row(i):
        return (i, 0)

    def folded(dil):
        return pl.BlockSpec((1, dil, tm // dil, B_W), lambda i: (i // tiles, 0, i % tiles, 0))

    out_shape = [jax.ShapeDtypeStruct((n_tok, A_Q), BF16),
                 jax.ShapeDtypeStruct((n_tok, A_KV), BF16),
                 jax.ShapeDtypeStruct((bsz, A_KV, seq), BF16)]
    out_specs = [pl.BlockSpec((tm, A_Q), row), pl.BlockSpec((tm, A_KV), row),
                 pl.BlockSpec((1, A_KV, tm), lambda i: (i // tiles, 0, i % tiles))]
    for dil in B_DILATIONS:
        for _ in range(3):
            if dil == 1:
                out_shape.append(jax.ShapeDtypeStruct((n_tok, B_W), BF16))
                out_specs.append(pl.BlockSpec((tm, B_W), row))
            else:
                out_shape.append(jax.ShapeDtypeStruct((bsz, dil, seq // dil, B_W), BF16))
                out_specs.append(folded(dil))
    out_shape.append(jax.ShapeDtypeStruct((n_tok, 2 * D_MODEL), BF16))
    out_specs.append(pl.BlockSpec((tm, 2 * D_MODEL), row))

    return pl.pallas_call(
        _proj_kernel,
        grid=(n_tok // tm,),
        in_specs=[pl.BlockSpec((tm, D_MODEL), row),
                  _resident((1, D_MODEL)),
                  _resident((D_MODEL, D_IN)),
                  pl.BlockSpec((tm, HEAD_DIM), lambda i: (i % tiles, 0)),
                  pl.BlockSpec((tm, HEAD_DIM), lambda i: (i % tiles, 0))],
        out_specs=out_specs,
        out_shape=out_shape,
        scratch_shapes=[pltpu.VMEM((B_HEADS, tm, HEAD_DIM), F32),
                        pltpu.VMEM((B_HEADS, FOLD_STRIDE, tm // FOLD_STRIDE, HEAD_DIM), F32)],
        compiler_params=pltpu.CompilerParams(dimension_semantics=("arbitrary",),
                                             vmem_limit_bytes=VMEM_LIMIT),
        name="proj",
    )(x2, g_pre, w_in, cosf, sinf)


def _band_bias(n_keys, n_queries, half_window, step):
    rel = (jnp.arange(n_keys)[None, :, None] - jnp.arange(n_queries)[None, None, :]
           - step * jnp.arange(3)[:, None, None])
    return jnp.where(jnp.abs(rel) <= half_window, 0.0, NEG).astype(F32)


def _attn_a_kernel(sink_ref, bias_ref, q_ref, k_ref, vt_ref, o_ref):
    seq = k_ref.shape[1]
    step_q0 = pl.program_id(1) * (A_STEP_BLOCKS * A_QBLOCK)

    chunks = [(blk, j, [j * A_GROUP + c * A_CHUNK_HEADS + g for g in range(A_CHUNK_HEADS)])
              for blk in range(A_STEP_BLOCKS) for j in range(A_KV_HEADS)
              for c in range(A_GROUP // A_CHUNK_HEADS)]

    def window(blk):
        q0 = step_q0 + blk * A_QBLOCK
        kstart = pl.multiple_of(jnp.clip(q0 - A_QBLOCK, 0, seq - A_KBAND), A_QBLOCK)
        return q0, kstart

    def scores(blk, j, hs):
        _, kstart = window(blk)
        rows = slice(blk * A_QBLOCK, (blk + 1) * A_QBLOCK)
        kb = k_ref[0, pl.ds(kstart, A_KBAND), j * HEAD_DIM:(j + 1) * HEAD_DIM]
        qs = jnp.concatenate([q_ref[0, rows, h * HEAD_DIM:(h + 1) * HEAD_DIM] for h in hs],
                             axis=0)
        return lax.dot_general(kb, qs, (((1,), (1,)), ((), ())), preferred_element_type=F32)

    def softmax(st, blk, hs):
        q0, kstart = window(blk)
        bias = bias_ref[(q0 - kstart) // A_QBLOCK]
        st = st + jnp.concatenate([bias] * A_CHUNK_HEADS, axis=1)
        sk = jnp.concatenate([jnp.full((1, A_QBLOCK), sink_ref[h] * LOG2E, F32) for h in hs],
                             axis=1)
        m = jnp.maximum(jnp.max(st, axis=0, keepdims=True), sk)
        pt = jnp.exp2(st - m)
        den = jnp.sum(pt, axis=0, keepdims=True) + jnp.exp2(sk - m)
        return pt.astype(BF16), 1.0 / den

    def output(blk, j, hs, pt, rden):
        _, kstart = window(blk)
        rows = slice(blk * A_QBLOCK, (blk + 1) * A_QBLOCK)
        vtb = vt_ref[0, j * HEAD_DIM:(j + 1) * HEAD_DIM, pl.ds(kstart, A_KBAND)]
        ot = jnp.dot(vtb, pt, preferred_element_type=F32) * rden
        for g, h in enumerate(hs):
            o_ref[0, rows, h * HEAD_DIM:(h + 1) * HEAD_DIM] = (
                ot[:, g * A_QBLOCK:(g + 1) * A_QBLOCK].T.astype(BF16))

    n = len(chunks)
    st = {0: scores(*chunks[0])}
    sm = {}
    for c in range(n):
        if c + 1 < n:
            st[c + 1] = scores(*chunks[c + 1])
        sm[c] = softmax(st.pop(c), chunks[c][0], chunks[c][2])
        if c >= 1:
            output(*chunks[c - 1], *sm.pop(c - 1))
    output(*chunks[n - 1], *sm.pop(n - 1))


def _attention_a(sink, qa, ka, vat, bsz, seq):
    q3 = qa.reshape(bsz, seq, A_Q)
    k3 = ka.reshape(bsz, seq, A_KV)
    v3 = vat
    return pl.pallas_call(
        _attn_a_kernel,
        grid=(bsz, seq // (A_STEP_BLOCKS * A_QBLOCK)),
        in_specs=[pl.BlockSpec(memory_space=pltpu.SMEM),
                  _resident((3, A_KBAND, A_QBLOCK)),
                  pl.BlockSpec((1, A_STEP_BLOCKS * A_QBLOCK, A_Q), lambda b, i: (b, i, 0)),
                  pl.BlockSpec((1, seq, A_KV), lambda b, i: (b, 0, 0)),
                  pl.BlockSpec((1, A_KV, seq), lambda b, i: (b, 0, 0))],
        out_specs=pl.BlockSpec((1, A_STEP_BLOCKS * A_QBLOCK, A_Q), lambda b, i: (b, i, 0)),
        out_shape=jax.ShapeDtypeStruct((bsz, seq, A_Q), BF16),
        compiler_params=pltpu.CompilerParams(dimension_semantics=("arbitrary", "arbitrary"),
                                             vmem_limit_bytes=VMEM_LIMIT),
        name="attn_a",
    )(sink, _band_bias(A_KBAND, A_QBLOCK, A_HALF_WINDOW, A_QBLOCK), q3, k3, v3)


def _b_geometry(seq, dil):
    ld = seq // dil
    qb = min(B_QBLOCK, ld)
    kw = min(qb + 2 * B_HALF_WINDOW, ld)
    return ld, qb, kw, ld // qb


def _attn_b_kernel(bias0_ref, bias1_ref, bias2_ref,
                   q0_ref, k0_ref, v0_ref, q1_ref, k1_ref, v1_ref, q2_ref, k2_ref, v2_ref,
                   o_ref, acc_ref, lse_ref):
    seq = o_ref.shape[1]
    group_refs = ((bias0_ref, q0_ref, k0_ref, v0_ref), (bias1_ref, q1_ref, k1_ref, v1_ref),
                  (bias2_ref, q2_ref, k2_ref, v2_ref))

    for g, dil in enumerate(B_DILATIONS):
        bias_ref, q_ref, k_ref, v_ref = group_refs[g]
        ld, qb, kw, nq = _b_geometry(seq, dil)

        n_blocks = dil * nq
        per_iter = min(B_UNROLL, n_blocks)

        def window(t, qb=qb, kw=kw, nq=nq, ld=ld):
            r = t // nq
            q0 = pl.multiple_of((t % nq) * qb, qb)
            kstart = pl.multiple_of(jnp.clip(q0 - B_HALF_WINDOW, 0, ld - kw), B_HALF_WINDOW)
            return r, q0, kstart

        def scores(t, window=window, qb=qb, kw=kw, q_ref=q_ref, k_ref=k_ref):
            r, q0, kstart = window(t)
            q = q_ref[0, r, pl.ds(q0, qb), :]
            kb = k_ref[0, r, pl.ds(kstart, kw), :]
            return lax.dot_general(kb, q, (((1,), (1,)), ((), ())), preferred_element_type=F32)

        def softmax(st, t, window=window, bias_ref=bias_ref):
            _, q0, kstart = window(t)
            st = st + bias_ref[(q0 - kstart) // B_HALF_WINDOW]
            m = jnp.max(st, axis=0, keepdims=True)
            pt = jnp.exp2(st - m)
            den = jnp.sum(pt, axis=0, keepdims=True)
            return pt.astype(BF16), 1.0 / den, m + jnp.log2(den)

        def output(t, pt, rden, lse2, window=window, dil=dil, qb=qb, kw=kw, v_ref=v_ref, g=g):
            r, q0, kstart = window(t)
            vb = v_ref[0, r, pl.ds(kstart, kw), :]
            ot = lax.dot_general(vb, pt, (((0,), (0,)), ((), ())),
                                 preferred_element_type=F32) * rden
            if dil == 1:
                rows = pl.ds(q0, qb)
            else:
                rows = pl.ds(q0 * dil + r, qb, stride=dil)
            acc_ref[g, rows, :] = ot.T
            lse_ref[g, rows, :] = jnp.broadcast_to(lse2, (HEAD_DIM, qb)).T

        def blocks(it, carry, per_iter=per_iter, scores=scores, softmax=softmax, output=output):
            ts = [it * per_iter + u for u in range(per_iter)]
            st = {0: scores(ts[0])}
            sm = {}
            for u in range(per_iter):
                if u + 1 < per_iter:
                    st[u + 1] = scores(ts[u + 1])
                sm[u] = softmax(st.pop(u), ts[u])
                if u >= 1:
                    output(ts[u - 1], *sm.pop(u - 1))
            output(ts[per_iter - 1], *sm.pop(per_iter - 1))
            return carry

        if n_blocks == per_iter:
            blocks(0, 0)
        else:
            lax.fori_loop(0, n_blocks // per_iter, blocks, 0)

    def merge(t, carry):
        rows = pl.ds(pl.multiple_of(t * MERGE_ROWS, MERGE_ROWS), MERGE_ROWS)
        ls = [lse_ref[g, rows, :] for g in range(N_B)]
        top = jnp.maximum(jnp.maximum(ls[0], ls[1]), ls[2])
        es = [jnp.exp2(l - top) for l in ls]
        rtot = 1.0 / (es[0] + es[1] + es[2])
        out = sum(es[g] * acc_ref[g, rows, :] for g in range(N_B)) * rtot
        o_ref[0, rows, :] = out.astype(BF16)
        return carry

    lax.fori_loop(0, seq // MERGE_ROWS, merge, 0)


def _attention_b(qkv, bsz, seq):
    args, in_specs = [], []
    for dil in B_DILATIONS:
        _, qb, kw, _ = _b_geometry(seq, dil)
        args.append(_band_bias(kw, qb, B_HALF_WINDOW, B_HALF_WINDOW))
        in_specs.append(_resident((3, kw, qb)))
    for g, dil in enumerate(B_DILATIONS):
        ld = seq // dil
        for part in range(3):
            args.append(qkv[3 * g + part].reshape(bsz, dil, ld, B_W))
            in_specs.append(pl.BlockSpec((1, dil, ld, HEAD_DIM), lambda b, h: (b, 0, 0, h)))
    return pl.pallas_call(
        _attn_b_kernel,
        grid=(bsz, B_HEADS),
        in_specs=in_specs,
        out_specs=pl.BlockSpec((1, seq, HEAD_DIM), lambda b, h: (b, 0, h)),
        out_shape=jax.ShapeDtypeStruct((bsz, seq, B_W), BF16),
        scratch_shapes=[pltpu.VMEM((N_B, seq, HEAD_DIM), F32),
                        pltpu.VMEM((N_B, seq, HEAD_DIM), F32)],
        compiler_params=pltpu.CompilerParams(dimension_semantics=("arbitrary", "arbitrary"),
                                             vmem_limit_bytes=VMEM_LIMIT),
        name="attn_b",
    )(*args)


def _post_kernel(x_ref, oa_ref, ob_ref, gate_ref, wa_ref, wb_ref, wo_ref, w1_ref, w2_ref,
                 g_mix_ref, g_pre_ref, g_post_ref, y_ref):
    ya = jnp.dot(oa_ref[...], wa_ref[...], preferred_element_type=F32)
    yb = jnp.dot(ob_ref[...], wb_ref[...], preferred_element_type=F32)
    ga = gate_ref[:, :D_MODEL].astype(F32)
    gb = gate_ref[:, D_MODEL:].astype(F32)
    mix_in = jax.nn.sigmoid(ga) * ya + jax.nn.sigmoid(gb) * yb
    mix = jnp.dot(mix_in.astype(BF16), wo_ref[...], preferred_element_type=F32)
    x1 = x_ref[...] + _rms(mix, g_mix_ref[...])
    h2 = _rms(x1, g_pre_ref[...]).astype(BF16)
    acc = jnp.zeros_like(x1)
    ff_chunk = D_MODEL
    for c in range(D_FF // ff_chunk):
        cols = slice(c * ff_chunk, (c + 1) * ff_chunk)
        u = jnp.maximum(jnp.dot(h2, w1_ref[:, cols], preferred_element_type=F32), 0.0)
        acc = acc + jnp.dot((u * u).astype(BF16), w2_ref[cols, :], preferred_element_type=F32)
    y_ref[...] = x1 + _rms(acc, g_post_ref[...])


def _post(x, oa, ob, gates, w_a, w_b, w_o, w_1, w_2, g_mix, g_pre, g_post):
    bsz, seq, _ = x.shape
    n_tok = bsz * seq
    tm = POST_TM

    def row(i):
        return (i, 0)

    y = pl.pallas_call(
        _post_kernel,
        grid=(n_tok // tm,),
        in_specs=[pl.BlockSpec((tm, D_MODEL), row),
                  pl.BlockSpec((tm, A_Q), row),
                  pl.BlockSpec((tm, B_W), row),
                  pl.BlockSpec((tm, 2 * D_MODEL), row),
                  _resident((A_Q, D_MODEL)), _resident((B_W, D_MODEL)),
                  _resident((D_MODEL, D_MODEL)),
                  _resident((D_MODEL, D_FF)), _resident((D_FF, D_MODEL)),
                  _resident((1, D_MODEL)), _resident((1, D_MODEL)), _resident((1, D_MODEL))],
        out_specs=pl.BlockSpec((tm, D_MODEL), row),
        out_shape=jax.ShapeDtypeStruct((n_tok, D_MODEL), F32),
        compiler_params=pltpu.CompilerParams(dimension_semantics=("arbitrary",),
                                             vmem_limit_bytes=VMEM_LIMIT),
        name="post",
    )(x.reshape(n_tok, D_MODEL), oa.reshape(n_tok, A_Q), ob.reshape(n_tok, B_W), gates,
      w_a, w_b, w_o, w_1, w_2, g_mix, g_pre, g_post)
    return y.reshape(bsz, seq, D_MODEL)


def _rope_tables(seq):
    half = HEAD_DIM // 2
    inv = jnp.power(ROPE_THETA, -jnp.arange(half, dtype=F32) / half)
    ang = jnp.arange(seq).astype(F32)[:, None] * inv[None, :]
    cos, sin = jnp.cos(ang), jnp.sin(ang)
    return jnp.concatenate([cos, cos], axis=-1), jnp.concatenate([-sin, sin], axis=-1)


def _layer(x, w_in, sink, w_a, w_b, w_o, g_pre_mix, g_post_mix, g_pre_mlp, g_post_mlp, w_1, w_2):
    bsz, seq, _ = x.shape
    cosf, sinf = _rope_tables(seq)
    outs = _project(x, g_pre_mix.reshape(1, D_MODEL), w_in, cosf, sinf)
    qa, ka, va = outs[:3]
    qkv_b = outs[3:12]
    gates = outs[12]
    oa = _attention_a(sink, qa, ka, va, bsz, seq)
    ob = _attention_b(qkv_b, bsz, seq)
    return _post(x, oa, ob, gates, w_a, w_b, w_o, w_1, w_2,
                 g_post_mix.reshape(1, D_MODEL), g_pre_mlp.reshape(1, D_MODEL),
                 g_post_mlp.reshape(1, D_MODEL))


def kernel(x_prompt, x_sample, w_in, sink, w_a, w_b, w_o, g_pre_mix, g_post_mix, g_pre_mlp,
           g_post_mlp, w_1, w_2):
    depth = w_in.shape[0]
    xs = [x_prompt, x_sample]
    for l in range(depth):
        ws = (w_in[l].astype(BF16), sink[l], w_a[l].astype(BF16), w_b[l].astype(BF16),
              w_o[l].astype(BF16), g_pre_mix[l], g_post_mix[l], g_pre_mlp[l], g_post_mlp[l],
              w_1[l].astype(BF16), w_2[l].astype(BF16))
        xs = [_layer(x, *ws) for x in xs]
    return (xs[0], xs[1])
```
